```python
import jax, jax.numpy as jnp
from jax import lax
import numpy as np

D_MODEL = 2048
BATCH = 1
SEQ = 8192
DEPTH = 4

N_MIXERS = 3
CONV_W = 3
SC_DIM = D_MODEL
CHUNK = 128
SG_DIM = D_MODEL
SG_GROUPS = 8
SB_HEADS = 16
SB_HEAD_DIM = D_MODEL // SB_HEADS
Q_BLOCK = 128
D_FF = 5632
EPS = 1e-6
N_A = len(range(0, DEPTH, N_MIXERS))
N_B = len(range(1, DEPTH, N_MIXERS))
N_C = len(range(2, DEPTH, N_MIXERS))

kernel_name = "hybrid_shortconv_sgu_stickbreak_trunk"


def rmsnorm(x, g):
    xf = x.astype(jnp.float32)
    y = xf * lax.rsqrt(jnp.mean(xf * xf, axis=-1, keepdims=True) + EPS)
    return (y * g.astype(jnp.float32)).astype(x.dtype)


def layernorm(x, g, b):
    xf = x.astype(jnp.float32)
    mu = jnp.mean(xf, axis=-1, keepdims=True)
    var = jnp.mean(jnp.square(xf - mu), axis=-1, keepdims=True)
    y = (xf - mu) * lax.rsqrt(var + EPS)
    return (y * g.astype(jnp.float32) + b.astype(jnp.float32)).astype(x.dtype)


def causal_dwconv(x, w, b=None):
    s = x.shape[1]
    xp = jnp.pad(x, ((0, 0), (CONV_W - 1, 0), (0, 0)))
    y = w[0] * xp[:, 0:s]
    for k in range(1, CONV_W):
        y = y + w[k] * xp[:, k:k + s]
    if b is not None:
        y = y + b
    return y


def short_conv_mixer(x, w_in, w_conv, w_out):
    gb, gc, h = jnp.split(x @ w_in, 3, axis=-1)
    y = gb * causal_dwconv(gc * h, w_conv)
    return y @ w_out


def chunked_sgu_mixer(x, w_in, ln_g, ln_b, w_s, b_s, w_out):
    bn, s, _ = x.shape
    h = jax.nn.gelu(x @ w_in, approximate=False)
    u, v = jnp.split(h, 2, axis=-1)
    v = layernorm(v, ln_g, ln_b)
    v = v.reshape(bn, s // CHUNK, CHUNK, SG_GROUPS, SG_DIM // SG_GROUPS)
    mask = jnp.tril(jnp.ones((CHUNK, CHUNK), dtype=bool))
    ws = jnp.where(mask[None], w_s, 0.0).astype(v.dtype)
    mixed = jnp.einsum('gts,bnsgc->bntgc', ws, v) + b_s.T[None, None, :, :, None].astype(v.dtype)
    y = u * mixed.reshape(bn, s, SG_DIM)
    return y @ w_out


def stick_breaking_mixer(x, w_qkv, w_out):
    bn, s, _ = x.shape
    qkv = (x @ w_qkv).reshape(bn, s, 3, SB_HEADS, SB_HEAD_DIM)
    q = qkv[:, :, 0].transpose(0, 2, 1, 3)
    k = qkv[:, :, 1].transpose(0, 2, 1, 3)
    v = qkv[:, :, 2].transpose(0, 2, 1, 3)
    nb = s // Q_BLOCK
    q_blocks = q.reshape(bn, SB_HEADS, nb, Q_BLOCK, SB_HEAD_DIM).transpose(2, 0, 1, 3, 4)
    kpos = jnp.arange(s)
    scale = SB_HEAD_DIM ** -0.5

    def block(args):
        qb, i = args
        qpos = i * Q_BLOCK + jnp.arange(Q_BLOCK)
        mask = kpos[None, :] < qpos[:, None]
        z = jnp.einsum('bhqd,bhkd->bhqk', qb, k).astype(jnp.float32) * scale
        log_1mb = jnp.where(mask, jax.nn.log_sigmoid(-z), 0.0)
        acc = lax.cumsum(log_1mb, axis=3, reverse=True) - log_1mb
        a = jnp.where(mask, jnp.exp(jax.nn.log_sigmoid(z) + acc), 0.0)
        return jnp.einsum('bhqk,bhkd->bhqd', a.astype(v.dtype), v)

    o = lax.map(block, (q_blocks, jnp.arange(nb)))
    o = o.transpose(1, 0, 3, 2, 4).reshape(bn, s, SB_HEADS * SB_HEAD_DIM)
    return o @ w_out


def conv_glu_ffn(x, w_up, conv_w, conv_b, w_down):
    h = causal_dwconv(x @ w_up, conv_w, conv_b)
    g, val = jnp.split(h, 2, axis=-1)
    return (jax.nn.silu(g) * val) @ w_down


def setup_inputs(seed: int = 0) -> dict:
    key = jax.random.key(seed)
    ks = jax.random.split(key, 20)
    f32 = jnp.float32

    def w(k, shape, fan_in):
        return jax.random.normal(k, shape, f32) * (fan_in ** -0.5)

    def gain(k, shape):
        return 1.0 + 0.05 * jax.random.normal(k, shape, f32)

    return {
        "x": jax.random.normal(ks[0], (BATCH, SEQ, D_MODEL), f32),
        "norm_mix_pre": gain(ks[1], (DEPTH, D_MODEL)),
        "norm_mix_post": gain(ks[2], (DEPTH, D_MODEL)),
        "norm_ffn_pre": gain(ks[3], (DEPTH, D_MODEL)),
        "norm_ffn_post": gain(ks[4], (DEPTH, D_MODEL)),
        "sc_w_in": w(ks[5], (N_A, D_MODEL, 3 * SC_DIM), D_MODEL),
        "sc_conv_w": w(ks[6], (N_A, CONV_W, SC_DIM), CONV_W),
        "sc_w_out": w(ks[7], (N_A, SC_DIM, D_MODEL), SC_DIM),
        "sg_w_in": w(ks[8], (N_B, D_MODEL, 2 * SG_DIM), D_MODEL),
        "sg_ln_g": gain(ks[9], (N_B, SG_DIM)),
        "sg_ln_b": 0.02 * jax.random.normal(ks[10], (N_B, SG_DIM), f32),
        "sg_w_s": w(ks[11], (N_B, SG_GROUPS, CHUNK, CHUNK), CHUNK),
        "sg_b_s": 1.0 + 0.1 * jax.random.normal(ks[12], (N_B, SG_GROUPS, CHUNK), f32),
        "sg_w_out": w(ks[13], (N_B, SG_DIM, D_MODEL), SG_DIM),
        "sb_w_qkv": w(ks[14], (N_C, D_MODEL, 3 * SB_HEADS * SB_HEAD_DIM), D_MODEL),
        "sb_w_out": w(ks[15], (N_C, SB_HEADS * SB_HEAD_DIM, D_MODEL), SB_HEADS * SB_HEAD_DIM),
        "ffn_w_up": w(ks[16], (DEPTH, D_MODEL, 2 * D_FF), D_MODEL),
        "ffn_conv_w": w(ks[17], (DEPTH, CONV_W, 2 * D_FF), CONV_W),
        "ffn_conv_b": 0.02 * jax.random.normal(ks[18], (DEPTH, 2 * D_FF), f32),
        "ffn_w_down": w(ks[19], (DEPTH, D_FF, D_MODEL), D_FF),
    }


def reference(x, norm_mix_pre, norm_mix_post, norm_ffn_pre, norm_ffn_post,
              sc_w_in, sc_conv_w, sc_w_out,
              sg_w_in, sg_ln_g, sg_ln_b, sg_w_s, sg_b_s, sg_w_out,
              sb_w_qkv, sb_w_out,
              ffn_w_up, ffn_conv_w, ffn_conv_b, ffn_w_down):
    h = x
    for i in range(DEPTH):
        kind = i % N_MIXERS
        j = i // N_MIXERS
        hn = rmsnorm(h, norm_mix_pre[i])
        if kind == 0:
            m = short_conv_mixer(hn, sc_w_in[j], sc_conv_w[j], sc_w_out[j])
        elif kind == 1:
            m = chunked_sgu_mixer(hn, sg_w_in[j], sg_ln_g[j], sg_ln_b[j], sg_w_s[j], sg_b_s[j], sg_w_out[j])
        else:
            m = stick_breaking_mixer(hn, sb_w_qkv[j], sb_w_out[j])
        h = h + rmsnorm(m, norm_mix_post[i])
        f = conv_glu_ffn(rmsnorm(h, norm_ffn_pre[i]), ffn_w_up[i], ffn_conv_w[i], ffn_conv_b[i], ffn_w_down[i])
        h = h + rmsnorm(f, norm_ffn_post[i])
    return h
```

```python
import functools
import math

import jax
import jax.numpy as jnp
from jax import lax
from jax.experimental import pallas as pl
from jax.experimental.pallas import tpu as pltpu

EPS = 1e-6
CONV_TAPS = 3
SUBLANES = 8
LANES = 128
SGU_CHUNK = 128
SGU_GROUPS = 8
SB_HEADS = 16
SB_HEAD_DIM = 128
SB_TILE = 128
SB_HEAD_GROUP = 4
SB_EXP_ZERO = 105.0
MIB = 1024 * 1024
VMEM_LIMIT = 56 * MIB

F32 = jnp.float32
BF16 = jnp.bfloat16


def _params(n_axes, vmem=VMEM_LIMIT):
    return pltpu.CompilerParams(
        dimension_semantics=("arbitrary",) * n_axes, vmem_limit_bytes=vmem)


def _dot(a, b):
    return jnp.dot(a, b, preferred_element_type=F32)


def _rms(x, g):
    ms = jnp.mean(x * x, axis=-1, keepdims=True)
    return x * lax.rsqrt(ms + EPS) * g


def _rmsnorm_body(x_ref, g_ref, o_ref):
    o_ref[...] = _rms(x_ref[...], g_ref[...]).astype(o_ref.dtype)


def _rmsnorm(x, g, tm=512):
    s, d = x.shape
    return pl.pallas_call(
        _rmsnorm_body,
        grid=(s // tm,),
        in_specs=[pl.BlockSpec((tm, d), lambda m: (m, 0)),
                  pl.BlockSpec((1, d), lambda m: (0, 0))],
        out_specs=pl.BlockSpec((tm, d), lambda m: (m, 0)),
        out_shape=jax.ShapeDtypeStruct((s, d), BF16),
        compiler_params=_params(1),
        name="rmsnorm_first",
    )(x, g.reshape(1, d))


def _causal_conv(p, w_ref, buf_ref, is_first):
    tm = p.shape[0]

    @pl.when(is_first)
    def _():
        buf_ref[0:SUBLANES, :] = jnp.zeros((SUBLANES, p.shape[1]), F32)

    buf_ref[SUBLANES:SUBLANES + tm, :] = p
    y = w_ref[2:3, :] * p
    y = y + w_ref[1:2, :] * buf_ref[SUBLANES - 1:SUBLANES - 1 + tm, :]
    y = y + w_ref[0:1, :] * buf_ref[SUBLANES - 2:SUBLANES - 2 + tm, :]
    buf_ref[0:SUBLANES, :] = buf_ref[tm:tm + SUBLANES, :]
    return y


def _sc_in_body(x_ref, wb_ref, wc_ref, wh_ref, cw_ref, y_ref, buf_ref):
    x = x_ref[...]
    p = _dot(x, wc_ref[...]) * _dot(x, wh_ref[...])
    conv = _causal_conv(p, cw_ref, buf_ref, pl.program_id(1) == 0)
    y_ref[...] = (_dot(x, wb_ref[...]) * conv).astype(y_ref.dtype)


def _sc_in(xn, w_in, conv_w, tm=512, tn=512):
    s, d = xn.shape
    c = conv_w.shape[1]
    nb = c // tn
    wspec = lambda part: pl.BlockSpec((d, tn), lambda n, m: (0, n + part * nb))
    return pl.pallas_call(
        _sc_in_body,
        grid=(nb, s // tm),
        in_specs=[pl.BlockSpec((tm, d), lambda n, m: (m, 0)),
                  wspec(0), wspec(1), wspec(2),
                  pl.BlockSpec((CONV_TAPS, tn), lambda n, m: (0, n))],
        out_specs=pl.BlockSpec((tm, tn), lambda n, m: (m, n)),
        out_shape=jax.ShapeDtypeStruct((s, c), BF16),
        scratch_shapes=[pltpu.VMEM((tm + SUBLANES, tn), F32)],
        compiler_params=_params(2),
        name="sc_in",
    )(xn, w_in, w_in, w_in, conv_w)


def _ffn_up_body(x_ref, wg_ref, wv_ref, cwg_ref, cwv_ref, bg_ref, bv_ref, a_ref, bufg_ref, bufv_ref):
    x = x_ref[...]
    first = pl.program_id(1) == 0
    g = _causal_conv(_dot(x, wg_ref[...]), cwg_ref, bufg_ref, first) + bg_ref[...]
    v = _causal_conv(_dot(x, wv_ref[...]), cwv_ref, bufv_ref, first) + bv_ref[...]
    a_ref[...] = (g * jax.nn.sigmoid(g) * v).astype(a_ref.dtype)


def _ffn_up(xn, w_up, conv_w, conv_b, tm=512, tn=1408):
    s, d = xn.shape
    f = w_up.shape[1] // 2
    nb = f // tn
    col = lambda part: (lambda n, m: (0, n + part * nb))
    return pl.pallas_call(
        _ffn_up_body,
        grid=(nb, s // tm),
        in_specs=[pl.BlockSpec((tm, d), lambda n, m: (m, 0)),
                  pl.BlockSpec((d, tn), col(0)), pl.BlockSpec((d, tn), col(1)),
                  pl.BlockSpec((CONV_TAPS, tn), col(0)), pl.BlockSpec((CONV_TAPS, tn), col(1)),
                  pl.BlockSpec((1, tn), col(0)), pl.BlockSpec((1, tn), col(1))],
        out_specs=pl.BlockSpec((tm, tn), lambda n, m: (m, n)),
        out_shape=jax.ShapeDtypeStruct((s, f), BF16),
        scratch_shapes=[pltpu.VMEM((tm + SUBLANES, tn), F32),
                        pltpu.VMEM((tm + SUBLANES, tn), F32)],
        compiler_params=_params(2),
        name="ffn_up",
    )(xn, w_up, w_up, conv_w, conv_w, conv_b.reshape(1, -1), conv_b.reshape(1, -1))


def _proj_body(x_ref, w_ref, o_ref, *, epilogue, scaled_blocks, scale):
    acc = _dot(x_ref[...], w_ref[...])
    if epilogue == "gelu":
        acc = 0.5 * acc * (1.0 + lax.erf(acc * (1.0 / math.sqrt(2.0))))
    elif epilogue == "qscale":
        acc = acc * jnp.where(pl.program_id(0) < scaled_blocks, scale, 1.0).astype(F32)
    o_ref[...] = acc.astype(o_ref.dtype)


def _proj(xn, w, epilogue, *, scaled_cols=0, scale=1.0, tm=512, tn=1024, name):
    s, d = xn.shape
    n = w.shape[1]
    body = functools.partial(_proj_body, epilogue=epilogue, scaled_blocks=scaled_cols // tn, scale=scale)
    return pl.pallas_call(
        body,
        grid=(n // tn, s // tm),
        in_specs=[pl.BlockSpec((tm, d), lambda j, m: (m, 0)),
                  pl.BlockSpec((d, tn), lambda j, m: (0, j))],
        out_specs=pl.BlockSpec((tm, tn), lambda j, m: (m, j)),
        out_shape=jax.ShapeDtypeStruct((s, n), BF16),
        compiler_params=_params(2),
        name=name,
    )(xn, w)


def _out_body(a_ref, w_ref, h_ref, gp_ref, gn_ref, ho_ref, xn_ref):
    m = _dot(a_ref[...], w_ref[...])
    h = h_ref[...] + _rms(m, gp_ref[...])
    ho_ref[...] = h
    xn_ref[...] = _rms(h, gn_ref[...]).astype(xn_ref.dtype)


def _out_last_body(a_ref, w_ref, h_ref, gp_ref, ho_ref):
    ho_ref[...] = h_ref[...] + _rms(_dot(a_ref[...], w_ref[...]), gp_ref[...])


def _out_proj(a, w, h, g_post, g_next, tm, name):
    s, k = a.shape
    d = w.shape[1]
    row = lambda width: pl.BlockSpec((tm, width), lambda m: (m, 0))
    vec = pl.BlockSpec((1, d), lambda m: (0, 0))
    w_spec = pl.BlockSpec((k, d), lambda m: (0, 0), pipeline_mode=pl.Buffered(1))
    if g_next is None:
        return pl.pallas_call(
            _out_last_body,
            grid=(s // tm,),
            in_specs=[row(k), w_spec, row(d), vec],
            out_specs=row(d),
            out_shape=jax.ShapeDtypeStruct((s, d), F32),
            compiler_params=_params(1),
            name=name + "_last",
        )(a, w, h, g_post.reshape(1, d)), None
    return pl.pallas_call(
        _out_body,
        grid=(s // tm,),
        in_specs=[row(k), w_spec, row(d), vec, vec],
        out_specs=[row(d), row(d)],
        out_shape=[jax.ShapeDtypeStruct((s, d), F32), jax.ShapeDtypeStruct((s, d), BF16)],
        compiler_params=_params(1),
        name=name,
    )(a, w, h, g_post.reshape(1, d), g_next.reshape(1, d))


def _sgu_body(u_ref, v_ref, g_ref, b_ref, ws_ref, bias_ref, y_ref):
    v = v_ref[...].astype(F32)
    mu = jnp.mean(v, axis=-1, keepdims=True)
    vc = v - mu
    var = jnp.mean(vc * vc, axis=-1, keepdims=True)
    vn = (vc * lax.rsqrt(var + EPS) * g_ref[...] + b_ref[...]).astype(BF16)
    t = SGU_CHUNK
    gw = v.shape[1] // SGU_GROUPS
    tril = (lax.broadcasted_iota(jnp.int32, (t, t), 0) >= lax.broadcasted_iota(jnp.int32, (t, t), 1))
    for g in range(SGU_GROUPS):
        ws = jnp.where(tril, ws_ref[g], 0.0).astype(BF16)
        cols = slice(g * gw, (g + 1) * gw)
        for c in range(v.shape[0] // t):
            rows = slice(c * t, (c + 1) * t)
            mixed = _dot(ws, vn[rows, cols]) + bias_ref[:, cols]
            y_ref[rows, cols] = (u_ref[rows, cols].astype(F32) * mixed).astype(y_ref.dtype)


def _sgu(h, ln_g, ln_b, w_s, bias_full, tm=256):
    s, two_d = h.shape
    d = two_d // 2
    vec = pl.BlockSpec((1, d), lambda m: (0, 0))
    return pl.pallas_call(
        _sgu_body,
        grid=(s // tm,),
        in_specs=[pl.BlockSpec((tm, d), lambda m: (m, 0)),
                  pl.BlockSpec((tm, d), lambda m: (m, 1)),
                  vec, vec,
                  pl.BlockSpec(w_s.shape, lambda m: (0, 0, 0)),
                  pl.BlockSpec(bias_full.shape, lambda m: (0, 0))],
        out_specs=pl.BlockSpec((tm, d), lambda m: (m, 0)),
        out_shape=jax.ShapeDtypeStruct((s, d), BF16),
        compiler_params=_params(1),
        name="sgu_mix",
    )(h, h, ln_g.reshape(1, d), ln_b.reshape(1, d), w_s, bias_full)


def _sb_tile(q, k_ref, v_ref, acc_ref, carry_ref, uo, j, hd, mask):
    t = SB_TILE
    cols = slice(hd * SB_HEAD_DIM, (hd + 1) * SB_HEAD_DIM)
    rows = pl.ds(pl.multiple_of(j * t, t), t)
    kb = k_ref[rows, cols]
    vb = v_ref[rows, cols]
    z = lax.dot_general(q, kb, (((1,), (1,)), ((), ())), preferred_element_type=F32)
    l = -(jnp.maximum(z, 0.0) + jnp.log1p(jnp.exp(-jnp.abs(z))))
    if mask is not None:
        l = jnp.where(mask, l, 0.0)
    l_hi = l.astype(BF16)
    l_lo = (l - l_hi.astype(F32)).astype(BF16)
    sums = _dot(l_hi, uo) + _dot(l_lo, uo)
    carry = carry_ref[hd]
    a = jnp.exp(z + l + sums[:, :t] + carry)
    if mask is not None:
        a = jnp.where(mask, a, 0.0)
    acc_ref[hd] += _dot(a.astype(BF16), vb)
    carry = carry + sums[:, t:]
    carry_ref[hd] = carry
    return jnp.max(carry)


def _sb_attn_body(q_ref, k_ref, v_ref, o_ref, acc_ref, carry_ref):
    t = SB_TILE
    i = pl.program_id(1)
    r = lax.broadcasted_iota(jnp.int32, (t, t), 0)
    c = lax.broadcasted_iota(jnp.int32, (t, t), 1)
    causal = c < r
    ur = lax.broadcasted_iota(jnp.int32, (t, t + LANES), 0)
    uc = lax.broadcasted_iota(jnp.int32, (t, t + LANES), 1)
    uo = jnp.where((ur > uc) | (uc >= t), 1.0, 0.0).astype(BF16)

    acc_ref[...] = jnp.zeros(acc_ref.shape, F32)
    carry_ref[...] = jnp.zeros(carry_ref.shape, F32)
    qs = [q_ref[:, hd * SB_HEAD_DIM:(hd + 1) * SB_HEAD_DIM] for hd in range(SB_HEAD_GROUP)]

    def visit(j, mask):
        cmax = None
        for hd in range(SB_HEAD_GROUP):
            m = _sb_tile(qs[hd], k_ref, v_ref, acc_ref, carry_ref, uo, j, hd, mask)
            cmax = m if cmax is None else jnp.maximum(cmax, m)
        return cmax

    cmax0 = visit(i, causal)

    def cond(state):
        j, cmax = state
        return jnp.logical_and(j >= 0, cmax > -SB_EXP_ZERO)

    def body(state):
        j, _ = state
        return j - 1, visit(j, None)

    lax.while_loop(cond, body, (i - 1, cmax0))
    for hd in range(SB_HEAD_GROUP):
        o_ref[:, hd * SB_HEAD_DIM:(hd + 1) * SB_HEAD_DIM] = acc_ref[hd].astype(o_ref.dtype)


def _sb_attn(qkv):
    s = qkv.shape[0]
    t = SB_TILE
    gw = SB_HEAD_GROUP * SB_HEAD_DIM
    ng = SB_HEADS // SB_HEAD_GROUP
    kv_spec = lambda part: pl.BlockSpec((s, gw), lambda g, i: (0, part * ng + g),
                                        pipeline_mode=pl.Buffered(1))
    return pl.pallas_call(
        _sb_attn_body,
        grid=(ng, s // t),
        in_specs=[pl.BlockSpec((t, gw), lambda g, i: (i, g)), kv_spec(1), kv_spec(2)],
        out_specs=pl.BlockSpec((t, gw), lambda g, i: (i, g)),
        out_shape=jax.ShapeDtypeStruct((s, SB_HEADS * SB_HEAD_DIM), BF16),
        scratch_shapes=[pltpu.VMEM((SB_HEAD_GROUP, t, SB_HEAD_DIM), F32),
                        pltpu.VMEM((SB_HEAD_GROUP, t, LANES), F32)],
        compiler_params=_params(2),
        name="sb_attn",
    )(qkv, qkv, qkv)


def kernel(x, norm_mix_pre, norm_mix_post, norm_ffn_pre, norm_ffn_post, sc_w_in, sc_conv_w, sc_w_out, sg_w_in, sg_ln_g, sg_ln_b, sg_w_s, sg_b_s, sg_w_out, sb_w_qkv, sb_w_out, ffn_w_up, ffn_conv_w, ffn_conv_b, ffn_w_down):
    batch, seq, d = x.shape
    assert batch == 1, "row tiles carry the causal conv state across the flattened sequence"
    depth = norm_mix_pre.shape[0]
    n_mixers = 3
    bf = lambda w: w.astype(BF16)

    h = x.reshape(seq, d)
    xn = _rmsnorm(h, norm_mix_pre[0])
    for i in range(depth):
        kind, j = i % n_mixers, i // n_mixers
        if kind == 0:
            a = _sc_in(xn, bf(sc_w_in[j]), sc_conv_w[j])
            w_out = sc_w_out[j]
        elif kind == 1:
            hid = _proj(xn, bf(sg_w_in[j]), "gelu", name="sg_in")
            bias_full = jnp.repeat(sg_b_s[j].T, hid.shape[1] // 2 // SGU_GROUPS, axis=1)
            a = _sgu(hid, sg_ln_g[j], sg_ln_b[j], sg_w_s[j], bias_full)
            w_out = sg_w_out[j]
        else:
            qkv = _proj(xn, bf(sb_w_qkv[j]), "qscale", scaled_cols=SB_HEADS * SB_HEAD_DIM,
                        scale=SB_HEAD_DIM ** -0.5, name="sb_qkv")
            a = _sb_attn(qkv)
            w_out = sb_w_out[j]
        h, xn = _out_proj(a, bf(w_out), h, norm_mix_post[i], norm_ffn_pre[i], 512, "mix_out")
        act = _ffn_up(xn, bf(ffn_w_up[i]), ffn_conv_w[i], ffn_conv_b[i])
        g_next = norm_mix_pre[i + 1] if i + 1 < depth else None
        h, xn = _out_proj(act, bf(ffn_w_down[i]), h, norm_ffn_post[i], g_next, 256, "ffn_down")
    return h.reshape(batch, seq, d)
```

```python
import functools
import math

import jax
import jax.numpy as jnp
from jax import lax
from jax.experimental import pallas as pl
from jax.experimental.pallas import tpu as pltpu

EPS = 1e-6
CONV_TAPS = 3
SUBLANES = 8
LANES = 128
MXU_COLS = 256
UNIT_ROWS = 512
ROW_CHUNK = 64
SGU_CHUNK = 128
SGU_GROUPS = 8
SB_HEADS = 16
SB_HEAD_DIM = 128
SB_TILE = 128
SB_HEAD_GROUP = 8
SB_WINDOW = 3
SB_EXP_ZERO = 105.0
MIB = 1024 * 1024
VMEM_LIMIT = 56 * MIB

F32 = jnp.float32
BF16 = jnp.bfloat16


def _params(n_axes, vmem=VMEM_LIMIT):
    return pltpu.CompilerParams(
        dimension_semantics=("arbitrary",) * n_axes, vmem_limit_bytes=vmem)


def _dot(a, b):
    return jnp.dot(a, b, preferred_element_type=F32)


def _rms(x, g):
    ms = jnp.mean(x * x, axis=-1, keepdims=True)
    return x * lax.rsqrt(ms + EPS) * g


def _rmsnorm_body(x_ref, g_ref, o_ref):
    o_ref[...] = _rms(x_ref[...], g_ref[...]).astype(o_ref.dtype)


def _rmsnorm(x, g, tm=512):
    s, d = x.shape
    return pl.pallas_call(
        _rmsnorm_body,
        grid=(s // tm,),
        in_specs=[pl.BlockSpec((tm, d), lambda m: (m, 0)),
                  pl.BlockSpec((1, d), lambda m: (0, 0))],
        out_specs=pl.BlockSpec((tm, d), lambda m: (m, 0)),
        out_shape=jax.ShapeDtypeStruct((s, d), BF16),
        compiler_params=_params(1),
        name="rmsnorm_first",
    )(x, g.reshape(1, d))


def _weight_spec(layer, d, tn, col_block):
    return pl.BlockSpec((None, d, tn), lambda n, m: (layer, 0, col_block(n)))


def _vec_spec(layer, rows, tn, col_block):
    return pl.BlockSpec((None, rows, tn), lambda n, m: (layer, 0, col_block(n)))


def _cast_weights_once(pairs):
    @pl.when(pl.program_id(1) == 0)
    def _():
        for src, dst in pairs:
            dst[...] = src[...].astype(dst.dtype)


def _units(tm, tn):
    return [(slice(r, r + UNIT_ROWS), slice(c, c + MXU_COLS))
            for c in range(0, tn, MXU_COLS) for r in range(0, tm, UNIT_ROWS)]


def _pipeline_units(units, issue, finish):
    issue(units[0], 0)
    for k, unit in enumerate(units):
        if k + 1 < len(units):
            issue(units[k + 1], (k + 1) % 2)
        finish(unit, k % 2)


def _conv_rows(raw_ref, tail_ref, cs, r0, w_ref):
    cur = raw_ref[r0:r0 + ROW_CHUNK, :]
    prev = tail_ref[:, cs] if r0 == 0 else raw_ref[r0 - SUBLANES:r0, :]
    ext = jnp.concatenate([prev, cur], axis=0)
    s1 = pltpu.roll(ext, 1, 0)[SUBLANES:]
    s2 = pltpu.roll(ext, 2, 0)[SUBLANES:]
    return w_ref[2:3, cs] * cur + w_ref[1:2, cs] * s1 + w_ref[0:1, cs] * s2


def _reset_tails(tail_refs):
    @pl.when(pl.program_id(1) == 0)
    def _():
        for t in tail_refs:
            t[...] = jnp.zeros(t.shape, t.dtype)


def _raw_slots(n):
    return [pltpu.VMEM((UNIT_ROWS, MXU_COLS), F32)] * n


def _sc_in_body(x_ref, wb_ref, wc_ref, wh_ref, cw_ref, y_ref, wbb_ref, wcb_ref, whb_ref, tail_ref,
                p0_ref, p1_ref, g0_ref, g1_ref):
    _cast_weights_once([(wb_ref, wbb_ref), (wc_ref, wcb_ref), (wh_ref, whb_ref)])
    _reset_tails([tail_ref])
    raw_p, raw_gate = (p0_ref, p1_ref), (g0_ref, g1_ref)

    def issue(unit, slot):
        rows, cs = unit
        x = x_ref[rows, :]
        raw_p[slot][...] = _dot(x, wcb_ref[:, cs]) * _dot(x, whb_ref[:, cs])
        raw_gate[slot][...] = _dot(x, wbb_ref[:, cs])

    def finish(unit, slot):
        rows, cs = unit
        for r0 in range(0, UNIT_ROWS, ROW_CHUNK):
            conv = _conv_rows(raw_p[slot], tail_ref, cs, r0, cw_ref)
            gate = raw_gate[slot][r0:r0 + ROW_CHUNK, :]
            y_ref[rows.start + r0:rows.start + r0 + ROW_CHUNK, cs] = (gate * conv).astype(y_ref.dtype)
        tail_ref[:, cs] = raw_p[slot][UNIT_ROWS - SUBLANES:, :]

    _pipeline_units(_units(*y_ref.shape), issue, finish)


def _sc_in(xn, w_in, conv_w, layer, tm=1024, tn=512):
    s, d = xn.shape
    c = conv_w.shape[-1]
    nb = c // tn
    part = lambda k: (lambda n: n + k * nb)
    return pl.pallas_call(
        _sc_in_body,
        grid=(nb, s // tm),
        in_specs=[pl.BlockSpec((tm, d), lambda n, m: (m, 0)),
                  _weight_spec(layer, d, tn, part(0)),
                  _weight_spec(layer, d, tn, part(1)),
                  _weight_spec(layer, d, tn, part(2)),
                  _vec_spec(layer, CONV_TAPS, tn, part(0))],
        out_specs=pl.BlockSpec((tm, tn), lambda n, m: (m, n)),
        out_shape=jax.ShapeDtypeStruct((s, c), BF16),
        scratch_shapes=[pltpu.VMEM((d, tn), BF16)] * 3 + [pltpu.VMEM((SUBLANES, tn), F32)] + _raw_slots(4),
        compiler_params=_params(2),
        name="sc_in",
    )(xn, w_in, w_in, w_in, conv_w)


def _ffn_up_body(x_ref, wg_ref, wv_ref, cwg_ref, cwv_ref, bg_ref, bv_ref, a_ref,
                 wgb_ref, wvb_ref, tailg_ref, tailv_ref, g0_ref, g1_ref, v0_ref, v1_ref):
    _cast_weights_once([(wg_ref, wgb_ref), (wv_ref, wvb_ref)])
    _reset_tails([tailg_ref, tailv_ref])
    raw_g, raw_v = (g0_ref, g1_ref), (v0_ref, v1_ref)

    def issue(unit, slot):
        rows, cs = unit
        x = x_ref[rows, :]
        raw_g[slot][...] = _dot(x, wgb_ref[:, cs])
        raw_v[slot][...] = _dot(x, wvb_ref[:, cs])

    def finish(unit, slot):
        rows, cs = unit
        for r0 in range(0, UNIT_ROWS, ROW_CHUNK):
            cg = _conv_rows(raw_g[slot], tailg_ref, cs, r0, cwg_ref) + bg_ref[:, cs]
            cv = _conv_rows(raw_v[slot], tailv_ref, cs, r0, cwv_ref) + bv_ref[:, cs]
            a_ref[rows.start + r0:rows.start + r0 + ROW_CHUNK, cs] = (
                cg * jax.nn.sigmoid(cg) * cv).astype(a_ref.dtype)
        tailg_ref[:, cs] = raw_g[slot][UNIT_ROWS - SUBLANES:, :]
        tailv_ref[:, cs] = raw_v[slot][UNIT_ROWS - SUBLANES:, :]

    _pipeline_units(_units(*a_ref.shape), issue, finish)


def _ffn_up(xn, w_up, conv_w, conv_b, layer, tm=1024, tn=512):
    s, d = xn.shape
    f = w_up.shape[-1] // 2
    nb = f // tn
    part = lambda k: (lambda n: n + k * nb)
    conv_b = conv_b.reshape(conv_b.shape[0], 1, 2 * f)
    return pl.pallas_call(
        _ffn_up_body,
        grid=(nb, s // tm),
        in_specs=[pl.BlockSpec((tm, d), lambda n, m: (m, 0)),
                  _weight_spec(layer, d, tn, part(0)), _weight_spec(layer, d, tn, part(1)),
                  _vec_spec(layer, CONV_TAPS, tn, part(0)), _vec_spec(layer, CONV_TAPS, tn, part(1)),
                  _vec_spec(layer, 1, tn, part(0)), _vec_spec(layer, 1, tn, part(1))],
        out_specs=pl.BlockSpec((tm, tn), lambda n, m: (m, n)),
        out_shape=jax.ShapeDtypeStruct((s, f), BF16),
        scratch_shapes=([pltpu.VMEM((d, tn), BF16)] * 2 + [pltpu.VMEM((SUBLANES, tn), F32)] * 2
                        + _raw_slots(4)),
        compiler_params=_params(2),
        name="ffn_up",
    )(xn, w_up, w_up, conv_w, conv_w, conv_b, conv_b)


def _proj_body(x_ref, w_ref, o_ref, wb_ref, r0_ref, r1_ref, *, epilogue, scaled_blocks, scale):
    _cast_weights_once([(w_ref, wb_ref)])
    raw = (r0_ref, r1_ref)
    if epilogue == "qscale":
        factor = jnp.where(pl.program_id(0) < scaled_blocks, scale, 1.0).astype(F32)

    def issue(unit, slot):
        rows, cs = unit
        raw[slot][...] = _dot(x_ref[rows, :], wb_ref[:, cs])

    def finish(unit, slot):
        rows, cs = unit
        for r0 in range(0, UNIT_ROWS, ROW_CHUNK):
            acc = raw[slot][r0:r0 + ROW_CHUNK, :]
            if epilogue == "gelu":
                acc = 0.5 * acc * (1.0 + lax.erf(acc * (1.0 / math.sqrt(2.0))))
            elif epilogue == "qscale":
                acc = acc * factor
            o_ref[rows.start + r0:rows.start + r0 + ROW_CHUNK, cs] = acc.astype(o_ref.dtype)

    _pipeline_units(_units(*o_ref.shape), issue, finish)


def _proj(xn, w, layer, epilogue, *, scaled_cols=0, scale=1.0, tm=1024, tn=1024, name):
    s, d = xn.shape
    n = w.shape[-1]
    body = functools.partial(_proj_body, epilogue=epilogue, scaled_blocks=scaled_cols // tn, scale=scale)
    return pl.pallas_call(
        body,
        grid=(n // tn, s // tm),
        in_specs=[pl.BlockSpec((tm, d), lambda j, m: (m, 0)),
                  _weight_spec(layer, d, tn, lambda j: j)],
        out_specs=pl.BlockSpec((tm, tn), lambda j, m: (m, j)),
        out_shape=jax.ShapeDtypeStruct((s, n), BF16),
        scratch_shapes=[pltpu.VMEM((d, tn), BF16)] + _raw_slots(2),
        compiler_params=_params(2),
        name=name,
    )(xn, w)


def _out_body(a_ref, w_ref, h_ref, gp_ref, gn_ref, ho_ref, xn_ref):
    m = _dot(a_ref[...], w_ref[...])
    h = h_ref[...] + _rms(m, gp_ref[...])
    ho_ref[...] = h
    xn_ref[...] = _rms(h, gn_ref[...]).astype(xn_ref.dtype)


def _out_last_body(a_ref, w_ref, h_ref, gp_ref, ho_ref):
    ho_ref[...] = h_ref[...] + _rms(_dot(a_ref[...], w_ref[...]), gp_ref[...])


def _out_proj(a, w, h, g_post, g_next, tm, name):
    s, k = a.shape
    d = w.shape[1]
    row = lambda width: pl.BlockSpec((tm, width), lambda m: (m, 0))
    vec = pl.BlockSpec((1, d), lambda m: (0, 0))
    w_spec = pl.BlockSpec((k, d), lambda m: (0, 0), pipeline_mode=pl.Buffered(1))
    if g_next is None:
        return pl.pallas_call(
            _out_last_body,
            grid=(s // tm,),
            in_specs=[row(k), w_spec, row(d), vec],
            out_specs=row(d),
            out_shape=jax.ShapeDtypeStruct((s, d), F32),
            compiler_params=_params(1),
            name=name + "_last",
        )(a, w, h, g_post.reshape(1, d)), None
    return pl.pallas_call(
        _out_body,
        grid=(s // tm,),
        in_specs=[row(k), w_spec, row(d), vec, vec],
        out_specs=[row(d), row(d)],
        out_shape=[jax.ShapeDtypeStruct((s, d), F32), jax.ShapeDtypeStruct((s, d), BF16)],
        compiler_params=_params(1),
        name=name,
    )(a, w, h, g_post.reshape(1, d), g_next.reshape(1, d))


def _sgu_body(u_ref, v_ref, g_ref, b_ref, ws_ref, bias_ref, y_ref):
    v = v_ref[...].astype(F32)
    mu = jnp.mean(v, axis=-1, keepdims=True)
    vc = v - mu
    var = jnp.mean(vc * vc, axis=-1, keepdims=True)
    vn = (vc * lax.rsqrt(var + EPS) * g_ref[...] + b_ref[...]).astype(BF16)
    t = SGU_CHUNK
    gw = v.shape[1] // SGU_GROUPS
    tril = (lax.broadcasted_iota(jnp.int32, (t, t), 0) >= lax.broadcasted_iota(jnp.int32, (t, t), 1))
    for g in range(SGU_GROUPS):
        ws = jnp.where(tril, ws_ref[g], 0.0).astype(BF16)
        cols = slice(g * gw, (g + 1) * gw)
        for c in range(v.shape[0] // t):
            rows = slice(c * t, (c + 1) * t)
            mixed = _dot(ws, vn[rows, cols]) + bias_ref[:, cols]
            y_ref[rows, cols] = (u_ref[rows, cols].astype(F32) * mixed).astype(y_ref.dtype)


def _sgu(h, ln_g, ln_b, w_s, bias_full, tm=256):
    s, two_d = h.shape
    d = two_d // 2
    vec = pl.BlockSpec((1, d), lambda m: (0, 0))
    return pl.pallas_call(
        _sgu_body,
        grid=(s // tm,),
        in_specs=[pl.BlockSpec((tm, d), lambda m: (m, 0)),
                  pl.BlockSpec((tm, d), lambda m: (m, 1)),
                  vec, vec,
                  pl.BlockSpec(w_s.shape, lambda m: (0, 0, 0)),
                  pl.BlockSpec(bias_full.shape, lambda m: (0, 0))],
        out_specs=pl.BlockSpec((tm, d), lambda m: (m, 0)),
        out_shape=jax.ShapeDtypeStruct((s, d), BF16),
        compiler_params=_params(1),
        name="sgu_mix",
    )(h, h, ln_g.reshape(1, d), ln_b.reshape(1, d), w_s, bias_full)


class _SbVisit:
    def __init__(self, q_ref, k_ref, v_ref, uo, hd, j, mask):
        t = SB_TILE
        self.cols = slice(hd * SB_HEAD_DIM, (hd + 1) * SB_HEAD_DIM)
        self.rows = pl.ds(pl.multiple_of(j * t, t), t)
        self.q_ref, self.k_ref, self.v_ref, self.uo, self.hd, self.mask = q_ref, k_ref, v_ref, uo, hd, mask

    def scores(self):
        q = self.q_ref[:, self.cols]
        kb = self.k_ref[self.rows, self.cols]
        self.z = lax.dot_general(q, kb, (((1,), (1,)), ((), ())), preferred_element_type=F32)

    def log_terms(self):
        z = self.z
        self.zl = jnp.minimum(z, 0.0) - jnp.log(1.0 + jnp.exp(-jnp.abs(z)))
        l = self.zl - z
        if self.mask is not None:
            l = jnp.where(self.mask, l, 0.0)
        self.l_hi = l.astype(BF16)
        self.l_lo = (l - self.l_hi.astype(F32)).astype(BF16)

    def tile_sums(self):
        self.sums = _dot(self.l_hi, self.uo) + _dot(self.l_lo, self.uo)

    def weights(self, carry):
        t = SB_TILE
        a = jnp.exp(self.zl + self.sums[:, :t] + carry)
        if self.mask is not None:
            a = jnp.where(self.mask, a, 0.0)
        self.a = a.astype(BF16)
        return carry + self.sums[:, t:]

    def values(self, acc):
        return acc + _dot(self.a, self.v_ref[self.rows, self.cols])

    STAGES = 5

    def run_stage(self, s, acc, carry):
        if s == 0:
            self.scores()
        elif s == 1:
            self.log_terms()
        elif s == 2:
            self.tile_sums()
        elif s == 3:
            carry[self.hd] = self.weights(carry[self.hd])
        else:
            acc[self.hd] = self.values(acc[self.hd])


def _sb_pipeline(visits, acc, carry):
    for step in range(len(visits) + _SbVisit.STAGES - 1):
        for s in reversed(range(_SbVisit.STAGES)):
            v = step - s
            if 0 <= v < len(visits):
                visits[v].run_stage(s, acc, carry)


def _sb_attn_body(q_ref, k_ref, v_ref, o_ref, acc_ref, carry_ref):
    t = SB_TILE
    i = pl.program_id(1)
    r = lax.broadcasted_iota(jnp.int32, (t, t), 0)
    c = lax.broadcasted_iota(jnp.int32, (t, t), 1)
    causal = c < r
    ur = lax.broadcasted_iota(jnp.int32, (t, t + LANES), 0)
    uc = lax.broadcasted_iota(jnp.int32, (t, t + LANES), 1)
    uo = jnp.where((ur > uc) | (uc >= t), 1.0, 0.0).astype(BF16)
    heads = range(SB_HEAD_GROUP)
    zeros = jnp.zeros((t, SB_HEAD_DIM), F32)

    def visit_tiles(tiles, acc, carry):
        visits = [_SbVisit(q_ref, k_ref, v_ref, uo, hd, j, mask) for j, mask in tiles for hd in heads]
        _sb_pipeline(visits, acc, carry)
        for hd in heads:
            acc_ref[hd], carry_ref[hd] = acc[hd], carry[hd]

    def fresh():
        return {hd: zeros for hd in heads}, {hd: zeros for hd in heads}

    windowed = i >= SB_WINDOW - 1

    @pl.when(windowed)
    def _():
        visit_tiles([(i - w, causal if w == 0 else None) for w in range(SB_WINDOW)], *fresh())

    @pl.when(jnp.logical_not(windowed))
    def _():
        visit_tiles([(i, causal)], *fresh())

    def cond(state):
        j, cmax = state
        return jnp.logical_and(j >= 0, cmax > -SB_EXP_ZERO)

    def body(state):
        j, _ = state
        visit_tiles([(j, None)], {hd: acc_ref[hd] for hd in heads}, {hd: carry_ref[hd] for hd in heads})
        return j - 1, jnp.max(carry_ref[...])

    first = jnp.where(windowed, i - SB_WINDOW, i - 1)
    lax.while_loop(cond, body, (first, jnp.max(carry_ref[...])))
    for hd in heads:
        o_ref[:, hd * SB_HEAD_DIM:(hd + 1) * SB_HEAD_DIM] = acc_ref[hd].astype(o_ref.dtype)


def _sb_attn(qkv):
    s = qkv.shape[0]
    t = SB_TILE
    gw = SB_HEAD_GROUP * SB_HEAD_DIM
    ng = SB_HEADS // SB_HEAD_GROUP
    kv_spec = lambda part: pl.BlockSpec((s, gw), lambda g, i: (0, part * ng + g),
                                        pipeline_mode=pl.Buffered(1))
    return pl.pallas_call(
        _sb_attn_body,
        grid=(ng, s // t),
        in_specs=[pl.BlockSpec((t, gw), lambda g, i: (i, g)), kv_spec(1), kv_spec(2)],
        out_specs=pl.BlockSpec((t, gw), lambda g, i: (i, g)),
        out_shape=jax.ShapeDtypeStruct((s, SB_HEADS * SB_HEAD_DIM), BF16),
        scratch_shapes=[pltpu.VMEM((SB_HEAD_GROUP, t, SB_HEAD_DIM), F32),
                        pltpu.VMEM((SB_HEAD_GROUP, t, LANES), F32)],
        compiler_params=_params(2),
        name="sb_attn",
    )(qkv, qkv, qkv)


def kernel(x, norm_mix_pre, norm_mix_post, norm_ffn_pre, norm_ffn_post, sc_w_in, sc_conv_w, sc_w_out, sg_w_in, sg_ln_g, sg_ln_b, sg_w_s, sg_b_s, sg_w_out, sb_w_qkv, sb_w_out, ffn_w_up, ffn_conv_w, ffn_conv_b, ffn_w_down):
    batch, seq, d = x.shape
    assert batch == 1, "row tiles carry the causal conv state across the flattened sequence"
    depth = norm_mix_pre.shape[0]
    n_mixers = 3
    bf = lambda w: w.astype(BF16)

    h = x.reshape(seq, d)
    xn = _rmsnorm(h, norm_mix_pre[0])
    for i in range(depth):
        kind, j = i % n_mixers, i // n_mixers
        if kind == 0:
            a = _sc_in(xn, sc_w_in, sc_conv_w, j)
            w_out = sc_w_out[j]
        elif kind == 1:
            hid = _proj(xn, sg_w_in, j, "gelu", name="sg_in")
            bias_full = jnp.repeat(sg_b_s[j].T, hid.shape[1] // 2 // SGU_GROUPS, axis=1)
            a = _sgu(hid, sg_ln_g[j], sg_ln_b[j], sg_w_s[j], bias_full)
            w_out = sg_w_out[j]
        else:
            qkv = _proj(xn, sb_w_qkv, j, "qscale", scaled_cols=SB_HEADS * SB_HEAD_DIM,
                        scale=SB_HEAD_DIM ** -0.5, name="sb_qkv")
            a = _sb_attn(qkv)
            w_out = sb_w_out[j]
        h, xn = _out_proj(a, bf(w_out), h, norm_mix_post[i], norm_ffn_pre[i], 512, "mix_out")
        act = _ffn_up(xn, ffn_w_up, ffn_conv_w, ffn_conv_b, i)
        g_next = norm_mix_pre[i + 1] if i + 1 < depth else None
        h, xn = _out_proj(act, bf(ffn_w_down[i]), h, norm_ffn_post[i], g_next, 256, "ffn_down")
    return h.reshape(batch, seq, d)
```

```python
import functools
import math

import jax
import jax.numpy as jnp
from jax import lax
from jax.experimental import pallas as pl
from jax.experimental.pallas import tpu as pltpu

EPS = 1e-6
CONV_TAPS = 3
SUBLANES = 8
LANES = 128
MXU_COLS = 256
UNIT_ROWS = 256
ROW_CHUNK = 64
SGU_CHUNK = 128
SGU_GROUPS = 8
SB_HEADS = 16
SB_HEAD_DIM = 128
SB_TILE = 128
SB_HEAD_GROUP = 8
SB_WINDOW = 3
SB_EXP_ZERO = 105.0
MIB = 1024 * 1024
VMEM_LIMIT = 56 * MIB

F32 = jnp.float32
BF16 = jnp.bfloat16


def _params(n_axes, vmem=VMEM_LIMIT):
    return pltpu.CompilerParams(
        dimension_semantics=("arbitrary",) * n_axes, vmem_limit_bytes=vmem)


def _dot(a, b):
    return jnp.dot(a, b, preferred_element_type=F32)


def _rms(x, g):
    ms = jnp.mean(x * x, axis=-1, keepdims=True)
    return x * lax.rsqrt(ms + EPS) * g


def _rmsnorm_body(x_ref, g_ref, o_ref):
    o_ref[...] = _rms(x_ref[...], g_ref[...]).astype(o_ref.dtype)


def _rmsnorm(x, g, tm=512):
    s, d = x.shape
    return pl.pallas_call(
        _rmsnorm_body,
        grid=(s // tm,),
        in_specs=[pl.BlockSpec((tm, d), lambda m: (m, 0)),
                  pl.BlockSpec((1, d), lambda m: (0, 0))],
        out_specs=pl.BlockSpec((tm, d), lambda m: (m, 0)),
        out_shape=jax.ShapeDtypeStruct((s, d), BF16),
        compiler_params=_params(1),
        name="rmsnorm_first",
    )(x, g.reshape(1, d))


def _weight_spec(layer, d, tn, col_block):
    return pl.BlockSpec((None, d, tn), lambda n, m: (layer, 0, col_block(n)))


def _vec_spec(layer, rows, tn, col_block):
    return pl.BlockSpec((None, rows, tn), lambda n, m: (layer, 0, col_block(n)))


def _cast_weights_once(pairs):
    @pl.when(pl.program_id(1) == 0)
    def _():
        for src, dst in pairs:
            dst[...] = src[...].astype(dst.dtype)


def _units(tm, tn):
    return [(slice(r, r + UNIT_ROWS), slice(c, c + MXU_COLS))
            for c in range(0, tn, MXU_COLS) for r in range(0, tm, UNIT_ROWS)]


def _pipeline_units(units, issue, finish):
    issue(units[0], 0)
    for k, unit in enumerate(units):
        if k + 1 < len(units):
            issue(units[k + 1], (k + 1) % 2)
        finish(unit, k % 2)


def _traced_zero():
    return jnp.minimum(pl.program_id(1), 0)


def _raw_rows(raw_ref, base, r0, n):
    return raw_ref[pl.ds(pl.multiple_of(base + r0, SUBLANES), n), :]


def _conv_rows(raw_ref, tail_ref, cs, base, r0, w_ref):
    cur = _raw_rows(raw_ref, base, r0, ROW_CHUNK)
    prev = tail_ref[:, cs] if r0 == 0 else _raw_rows(raw_ref, base, r0 - SUBLANES, SUBLANES)
    ext = jnp.concatenate([prev, cur], axis=0)
    s1 = pltpu.roll(ext, 1, 0)[SUBLANES:]
    s2 = pltpu.roll(ext, 2, 0)[SUBLANES:]
    return w_ref[2:3, cs] * cur + w_ref[1:2, cs] * s1 + w_ref[0:1, cs] * s2


def _reset_tails(tail_refs):
    @pl.when(pl.program_id(1) == 0)
    def _():
        for t in tail_refs:
            t[...] = jnp.zeros(t.shape, t.dtype)


def _raw_slots(n):
    return [pltpu.VMEM((UNIT_ROWS, MXU_COLS), F32)] * n


def _sc_in_body(x_ref, wb_ref, wc_ref, wh_ref, cw_ref, y_ref, wbb_ref, wcb_ref, whb_ref, tail_ref,
                p0_ref, p1_ref, g0_ref, g1_ref):
    _cast_weights_once([(wb_ref, wbb_ref), (wc_ref, wcb_ref), (wh_ref, whb_ref)])
    _reset_tails([tail_ref])
    raw_p, raw_gate = (p0_ref, p1_ref), (g0_ref, g1_ref)

    def issue(unit, slot):
        rows, cs = unit
        x = x_ref[rows, :]
        raw_p[slot][...] = _dot(x, wcb_ref[:, cs]) * _dot(x, whb_ref[:, cs])
        raw_gate[slot][...] = _dot(x, wbb_ref[:, cs])

    base = _traced_zero()

    def finish(unit, slot):
        rows, cs = unit
        for r0 in range(0, UNIT_ROWS, ROW_CHUNK):
            conv = _conv_rows(raw_p[slot], tail_ref, cs, base, r0, cw_ref)
            gate = _raw_rows(raw_gate[slot], base, r0, ROW_CHUNK)
            y_ref[rows.start + r0:rows.start + r0 + ROW_CHUNK, cs] = (gate * conv).astype(y_ref.dtype)
        tail_ref[:, cs] = _raw_rows(raw_p[slot], base, UNIT_ROWS - SUBLANES, SUBLANES)

    _pipeline_units(_units(*y_ref.shape), issue, finish)


def _sc_in(xn, w_in, conv_w, layer, tm=1024, tn=512):
    s, d = xn.shape
    c = conv_w.shape[-1]
    nb = c // tn
    part = lambda k: (lambda n: n + k * nb)
    return pl.pallas_call(
        _sc_in_body,
        grid=(nb, s // tm),
        in_specs=[pl.BlockSpec((tm, d), lambda n, m: (m, 0)),
                  _weight_spec(layer, d, tn, part(0)),
                  _weight_spec(layer, d, tn, part(1)),
                  _weight_spec(layer, d, tn, part(2)),
                  _vec_spec(layer, CONV_TAPS, tn, part(0))],
        out_specs=pl.BlockSpec((tm, tn), lambda n, m: (m, n)),
        out_shape=jax.ShapeDtypeStruct((s, c), BF16),
        scratch_shapes=[pltpu.VMEM((d, tn), BF16)] * 3 + [pltpu.VMEM((SUBLANES, tn), F32)] + _raw_slots(4),
        compiler_params=_params(2),
        name="sc_in",
    )(xn, w_in, w_in, w_in, conv_w)


def _ffn_up_body(x_ref, wg_ref, wv_ref, cwg_ref, cwv_ref, bg_ref, bv_ref, a_ref,
                 wgb_ref, wvb_ref, tailg_ref, tailv_ref, g0_ref, g1_ref, v0_ref, v1_ref):
    _cast_weights_once([(wg_ref, wgb_ref), (wv_ref, wvb_ref)])
    _reset_tails([tailg_ref, tailv_ref])
    raw_g, raw_v = (g0_ref, g1_ref), (v0_ref, v1_ref)

    def issue(unit, slot):
        rows, cs = unit
        x = x_ref[rows, :]
        raw_g[slot][...] = _dot(x, wgb_ref[:, cs])
        raw_v[slot][...] = _dot(x, wvb_ref[:, cs])

    base = _traced_zero()

    def finish(unit, slot):
        rows, cs = unit
        for r0 in range(0, UNIT_ROWS, ROW_CHUNK):
            cg = _conv_rows(raw_g[slot], tailg_ref, cs, base, r0, cwg_ref) + bg_ref[:, cs]
            cv = _conv_rows(raw_v[slot], tailv_ref, cs, base, r0, cwv_ref) + bv_ref[:, cs]
            a_ref[rows.start + r0:rows.start + r0 + ROW_CHUNK, cs] = (
                cg * jax.nn.sigmoid(cg) * cv).astype(a_ref.dtype)
        tailg_ref[:, cs] = _raw_rows(raw_g[slot], base, UNIT_ROWS - SUBLANES, SUBLANES)
        tailv_ref[:, cs] = _raw_rows(raw_v[slot], base, UNIT_ROWS - SUBLANES, SUBLANES)

    _pipeline_units(_units(*a_ref.shape), issue, finish)


def _ffn_up(xn, w_up, conv_w, conv_b, layer, tm=2048, tn=512):
    s, d = xn.shape
    f = w_up.shape[-1] // 2
    nb = f // tn
    part = lambda k: (lambda n: n + k * nb)
    conv_b = conv_b.reshape(conv_b.shape[0], 1, 2 * f)
    return pl.pallas_call(
        _ffn_up_body,
        grid=(nb, s // tm),
        in_specs=[pl.BlockSpec((tm, d), lambda n, m: (m, 0)),
                  _weight_spec(layer, d, tn, part(0)), _weight_spec(layer, d, tn, part(1)),
                  _vec_spec(layer, CONV_TAPS, tn, part(0)), _vec_spec(layer, CONV_TAPS, tn, part(1)),
                  _vec_spec(layer, 1, tn, part(0)), _vec_spec(layer, 1, tn, part(1))],
        out_specs=pl.BlockSpec((tm, tn), lambda n, m: (m, n)),
        out_shape=jax.ShapeDtypeStruct((s, f), BF16),
        scratch_shapes=([pltpu.VMEM((d, tn), BF16)] * 2 + [pltpu.VMEM((SUBLANES, tn), F32)] * 2
                        + _raw_slots(4)),
        compiler_params=_params(2),
        name="ffn_up",
    )(xn, w_up, w_up, conv_w, conv_w, conv_b, conv_b)


def _proj_body(x_ref, w_ref, o_ref, wb_ref, r0_ref, r1_ref, *, epilogue, scaled_blocks, scale):
    _cast_weights_once([(w_ref, wb_ref)])
    raw = (r0_ref, r1_ref)
    if epilogue == "qscale":
        factor = jnp.where(pl.program_id(0) < scaled_blocks, scale, 1.0).astype(F32)

    def issue(unit, slot):
        rows, cs = unit
        raw[slot][...] = _dot(x_ref[rows, :], wb_ref[:, cs])

    base = _traced_zero()

    def finish(unit, slot):
        rows, cs = unit
        for r0 in range(0, UNIT_ROWS, ROW_CHUNK):
            acc = _raw_rows(raw[slot], base, r0, ROW_CHUNK)
            if epilogue == "gelu":
                acc = 0.5 * acc * (1.0 + lax.erf(acc * (1.0 / math.sqrt(2.0))))
            elif epilogue == "qscale":
                acc = acc * factor
            o_ref[rows.start + r0:rows.start + r0 + ROW_CHUNK, cs] = acc.astype(o_ref.dtype)

    _pipeline_units(_units(*o_ref.shape), issue, finish)


def _proj(xn, w, layer, epilogue, *, scaled_cols=0, scale=1.0, tm=1024, tn=1024, name):
    s, d = xn.shape
    n = w.shape[-1]
    body = functools.partial(_proj_body, epilogue=epilogue, scaled_blocks=scaled_cols // tn, scale=scale)
    return pl.pallas_call(
        body,
        grid=(n // tn, s // tm),
        in_specs=[pl.BlockSpec((tm, d), lambda j, m: (m, 0)),
                  _weight_spec(layer, d, tn, lambda j: j)],
        out_specs=pl.BlockSpec((tm, tn), lambda j, m: (m, j)),
        out_shape=jax.ShapeDtypeStruct((s, n), BF16),
        scratch_shapes=[pltpu.VMEM((d, tn), BF16)] + _raw_slots(2),
        compiler_params=_params(2),
        name=name,
    )(xn, w)


OUT_ROW_CHUNK = 8
WEIGHT_STAGE_ROWS = 256


def _stage_weights(w_hbm, layer, wb_ref, stage_ref, sem):
    n = wb_ref.shape[0] // WEIGHT_STAGE_ROWS

    def copy(c):
        return pltpu.make_async_copy(
            w_hbm.at[layer, pl.ds(c * WEIGHT_STAGE_ROWS, WEIGHT_STAGE_ROWS), :],
            stage_ref.at[c % 2], sem.at[c % 2])

    copy(0).start()
    for c in range(n):
        if c + 1 < n:
            copy(c + 1).start()
        copy(c).wait()
        wb_ref[c * WEIGHT_STAGE_ROWS:(c + 1) * WEIGHT_STAGE_ROWS, :] = stage_ref[c % 2].astype(wb_ref.dtype)


def _out_body(a_ref, w_hbm, h_ref, gp_ref, *rest, layer, n_tiles, emit_next):
    if emit_next:
        gn_ref, ho_ref, xn_ref, wb_ref, stage_ref, sem, raw_ref = rest
    else:
        ho_ref, wb_ref, stage_ref, sem, raw_ref = rest
    s = pl.program_id(0)
    tm, d = ho_ref.shape

    @pl.when(s == 0)
    def _():
        _stage_weights(w_hbm, layer, wb_ref, stage_ref, sem)
        raw_ref[...] = jnp.zeros(raw_ref.shape, F32)

    base = jnp.minimum(s, 0)

    def epilogue():
        for r0 in range(0, tm, OUT_ROW_CHUNK):
            rows = slice(r0, r0 + OUT_ROW_CHUNK)
            h = h_ref[rows, :] + _rms(_raw_rows(raw_ref, base, r0, OUT_ROW_CHUNK), gp_ref[...])
            ho_ref[rows, :] = h
            if emit_next:
                xn_ref[rows, :] = _rms(h, gn_ref[...]).astype(xn_ref.dtype)

    @pl.when(s < n_tiles)
    def _():
        epilogue()
        for c in range(0, d, MXU_COLS):
            raw_ref[:, c:c + MXU_COLS] = _dot(a_ref[...], wb_ref[:, c:c + MXU_COLS])

    @pl.when(s == n_tiles)
    def _():
        epilogue()


def _out_proj(a, w, layer, h, g_post, g_next, tm, name):
    s, k = a.shape
    d = w.shape[-1]
    n_tiles = s // tm
    emit_next = g_next is not None
    prev_row = lambda width: pl.BlockSpec((tm, width), lambda i: (jnp.maximum(i - 1, 0), 0))
    vec = pl.BlockSpec((1, d), lambda i: (0, 0))
    in_specs = [pl.BlockSpec((tm, k), lambda i: (jnp.minimum(i, n_tiles - 1), 0)),
                pl.BlockSpec(memory_space=pl.ANY), prev_row(d), vec]
    args = [a, w, h, g_post.reshape(1, d)]
    out_specs, out_shape = [prev_row(d)], [jax.ShapeDtypeStruct((s, d), F32)]
    if emit_next:
        in_specs.append(vec)
        args.append(g_next.reshape(1, d))
        out_specs.append(prev_row(d))
        out_shape.append(jax.ShapeDtypeStruct((s, d), BF16))
    outs = pl.pallas_call(
        functools.partial(_out_body, layer=layer, n_tiles=n_tiles, emit_next=emit_next),
        grid=(n_tiles + 1,),
        in_specs=in_specs,
        out_specs=out_specs,
        out_shape=out_shape,
        scratch_shapes=[pltpu.VMEM((k, d), BF16),
                        pltpu.VMEM((2, WEIGHT_STAGE_ROWS, d), F32),
                        pltpu.SemaphoreType.DMA((2,)),
                        pltpu.VMEM((tm, d), F32)],
        compiler_params=_params(1),
        name=name if emit_next else name + "_last",
    )(*args)
    return (outs[0], outs[1]) if emit_next else (outs[0], None)


def _sgu_body(u_ref, v_ref, g_ref, b_ref, ws_ref, bias_ref, y_ref):
    v = v_ref[...].astype(F32)
    mu = jnp.mean(v, axis=-1, keepdims=True)
    vc = v - mu
    var = jnp.mean(vc * vc, axis=-1, keepdims=True)
    vn = (vc * lax.rsqrt(var + EPS) * g_ref[...] + b_ref[...]).astype(BF16)
    t = SGU_CHUNK
    gw = v.shape[1] // SGU_GROUPS
    tril = (lax.broadcasted_iota(jnp.int32, (t, t), 0) >= lax.broadcasted_iota(jnp.int32, (t, t), 1))
    for g in range(SGU_GROUPS):
        ws = jnp.where(tril, ws_ref[g], 0.0).astype(BF16)
        cols = slice(g * gw, (g + 1) * gw)
        for c in range(v.shape[0] // t):
            rows = slice(c * t, (c + 1) * t)
            mixed = _dot(ws, vn[rows, cols]) + bias_ref[:, cols]
            y_ref[rows, cols] = (u_ref[rows, cols].astype(F32) * mixed).astype(y_ref.dtype)


def _sgu(h, ln_g, ln_b, w_s, bias_full, tm=256):
    s, two_d = h.shape
    d = two_d // 2
    vec = pl.BlockSpec((1, d), lambda m: (0, 0))
    return pl.pallas_call(
        _sgu_body,
        grid=(s // tm,),
        in_specs=[pl.BlockSpec((tm, d), lambda m: (m, 0)),
                  pl.BlockSpec((tm, d), lambda m: (m, 1)),
                  vec, vec,
                  pl.BlockSpec(w_s.shape, lambda m: (0, 0, 0)),
                  pl.BlockSpec(bias_full.shape, lambda m: (0, 0))],
        out_specs=pl.BlockSpec((tm, d), lambda m: (m, 0)),
        out_shape=jax.ShapeDtypeStruct((s, d), BF16),
        compiler_params=_params(1),
        name="sgu_mix",
    )(h, h, ln_g.reshape(1, d), ln_b.reshape(1, d), w_s, bias_full)


class _SbVisit:
    def __init__(self, q_ref, k_ref, v_ref, uo, hd, j, mask):
        t = SB_TILE
        self.cols = slice(hd * SB_HEAD_DIM, (hd + 1) * SB_HEAD_DIM)
        self.rows = pl.ds(pl.multiple_of(j * t, t), t)
        self.q_ref, self.k_ref, self.v_ref, self.uo, self.hd, self.mask = q_ref, k_ref, v_ref, uo, hd, mask

    def scores(self):
        q = self.q_ref[:, self.cols]
        kb = self.k_ref[self.rows, self.cols]
        self.z = lax.dot_general(q, kb, (((1,), (1,)), ((), ())), preferred_element_type=F32)

    def log_terms(self):
        z = self.z
        self.zl = jnp.minimum(z, 0.0) - jnp.log(1.0 + jnp.exp(-jnp.abs(z)))
        l = self.zl - z
        if self.mask is not None:
            l = jnp.where(self.mask, l, 0.0)
        self.l_hi = l.astype(BF16)
        self.l_lo = (l - self.l_hi.astype(F32)).astype(BF16)

    def tile_sums(self):
        self.sums = _dot(self.l_hi, self.uo) + _dot(self.l_lo, self.uo)

    def weights(self, carry):
        t = SB_TILE
        a = jnp.exp(self.zl + self.sums[:, :t] + carry)
        if self.mask is not None:
            a = jnp.where(self.mask, a, 0.0)
        self.a = a.astype(BF16)
        return carry + self.sums[:, t:]

    def values(self, acc):
        return acc + _dot(self.a, self.v_ref[self.rows, self.cols])

    STAGES = 5

    def run_stage(self, s, acc, carry):
        if s == 0:
            self.scores()
        elif s == 1:
            self.log_terms()
        elif s == 2:
            self.tile_sums()
        elif s == 3:
            carry[self.hd] = self.weights(carry[self.hd])
        else:
            acc[self.hd] = self.values(acc[self.hd])


def _sb_pipeline(visits, acc, carry):
    for step in range(len(visits) + _SbVisit.STAGES - 1):
        for s in reversed(range(_SbVisit.STAGES)):
            v = step - s
            if 0 <= v < len(visits):
                visits[v].run_stage(s, acc, carry)


def _sb_attn_body(q_ref, k_ref, v_ref, o_ref, acc_ref, carry_ref):
    t = SB_TILE
    i = pl.program_id(1)
    r = lax.broadcasted_iota(jnp.int32, (t, t), 0)
    c = lax.broadcasted_iota(jnp.int32, (t, t), 1)
    causal = c < r
    ur = lax.broadcasted_iota(jnp.int32, (t, t + LANES), 0)
    uc = lax.broadcasted_iota(jnp.int32, (t, t + LANES), 1)
    uo = jnp.where((ur > uc) | (uc >= t), 1.0, 0.0).astype(BF16)
    heads = range(SB_HEAD_GROUP)
    zeros = jnp.zeros((t, SB_HEAD_DIM), F32)

    def visit_tiles(tiles, acc, carry):
        visits = [_SbVisit(q_ref, k_ref, v_ref, uo, hd, j, mask) for j, mask in tiles for hd in heads]
        _sb_pipeline(visits, acc, carry)
        for hd in heads:
            acc_ref[hd], carry_ref[hd] = acc[hd], carry[hd]

    def fresh():
        return {hd: zeros for hd in heads}, {hd: zeros for hd in heads}

    windowed = i >= SB_WINDOW - 1

    @pl.when(windowed)
    def _():
        visit_tiles([(i - w, causal if w == 0 else None) for w in range(SB_WINDOW)], *fresh())

    @pl.when(jnp.logical_not(windowed))
    def _():
        visit_tiles([(i, causal)], *fresh())

    def cond(state):
        j, cmax = state
        return jnp.logical_and(j >= 0, cmax > -SB_EXP_ZERO)

    def body(state):
        j, _ = state
        visit_tiles([(j, None)], {hd: acc_ref[hd] for hd in heads}, {hd: carry_ref[hd] for hd in heads})
        return j - 1, jnp.max(carry_ref[...])

    first = jnp.where(windowed, i - SB_WINDOW, i - 1)
    lax.while_loop(cond, body, (first, jnp.max(carry_ref[...])))
    for hd in heads:
        o_ref[:, hd * SB_HEAD_DIM:(hd + 1) * SB_HEAD_DIM] = acc_ref[hd].astype(o_ref.dtype)


def _sb_attn(qkv):
    s = qkv.shape[0]
    t = SB_TILE
    gw = SB_HEAD_GROUP * SB_HEAD_DIM
    ng = SB_HEADS // SB_HEAD_GROUP
    kv_spec = lambda part: pl.BlockSpec((s, gw), lambda g, i: (0, part * ng + g),
                                        pipeline_mode=pl.Buffered(1))
    return pl.pallas_call(
        _sb_attn_body,
        grid=(ng, s // t),
        in_specs=[pl.BlockSpec((t, gw), lambda g, i: (i, g)), kv_spec(1), kv_spec(2)],
        out_specs=pl.BlockSpec((t, gw), lambda g, i: (i, g)),
        out_shape=jax.ShapeDtypeStruct((s, SB_HEADS * SB_HEAD_DIM), BF16),
        scratch_shapes=[pltpu.VMEM((SB_HEAD_GROUP, t, SB_HEAD_DIM), F32),
                        pltpu.VMEM((SB_HEAD_GROUP, t, LANES), F32)],
        compiler_params=_params(2),
        name="sb_attn",
    )(qkv, qkv, qkv)


def kernel(x, norm_mix_pre, norm_mix_post, norm_ffn_pre, norm_ffn_post, sc_w_in, sc_conv_w, sc_w_out, sg_w_in, sg_ln_g, sg_ln_b, sg_w_s, sg_b_s, sg_w_out, sb_w_qkv, sb_w_out, ffn_w_up, ffn_conv_w, ffn_conv_b, ffn_w_down):
    batch, seq, d = x.shape
    assert batch == 1, "row tiles carry the causal conv state across the flattened sequence"
    depth = norm_mix_pre.shape[0]
    n_mixers = 3

    h = x.reshape(seq, d)
    xn = _rmsnorm(h, norm_mix_pre[0])
    for i in range(depth):
        kind, j = i % n_mixers, i // n_mixers
        if kind == 0:
            a = _sc_in(xn, sc_w_in, sc_conv_w, j)
            w_out = sc_w_out
        elif kind == 1:
            hid = _proj(xn, sg_w_in, j, "gelu", name="sg_in")
            bias_full = jnp.repeat(sg_b_s[j].T, hid.shape[1] // 2 // SGU_GROUPS, axis=1)
            a = _sgu(hid, sg_ln_g[j], sg_ln_b[j], sg_w_s[j], bias_full)
            w_out = sg_w_out
        else:
            qkv = _proj(xn, sb_w_qkv, j, "qscale", scaled_cols=SB_HEADS * SB_HEAD_DIM,
                        scale=SB_HEAD_DIM ** -0.5, name="sb_qkv")
            a = _sb_attn(qkv)
            w_out = sb_w_out
        h, xn = _out_proj(a, w_out, j, h, norm_mix_post[i], norm_ffn_pre[i], 512, "mix_out")
        act = _ffn_up(xn, ffn_w_up, ffn_conv_w, ffn_conv_b, i)
        g_next = norm_mix_pre[i + 1] if i + 1 < depth else None
        h, xn = _out_proj(act, ffn_w_down, i, h, norm_ffn_post[i], g_next, 256, "ffn_down")
    return h.reshape(batch, seq, d)
```

```python
import functools
import math

import jax
import jax.numpy as jnp
from jax import lax
from jax.experimental import pallas as pl
from jax.experimental.pallas import tpu as pltpu

EPS = 1e-6
CONV_TAPS = 3
SUBLANES = 8
LANES = 128
MXU_COLS = 256
UNIT_ROWS = 256
ROW_CHUNK = 64
SGU_CHUNK = 128
SGU_GROUPS = 8
SB_HEADS = 16
SB_HEAD_DIM = 128
SB_TILE = 256
SB_HEAD_GROUP = 8
SB_WINDOW = 2
SB_EXP_ZERO = 106.0
MIB = 1024 * 1024
VMEM_LIMIT = 56 * MIB

F32 = jnp.float32
BF16 = jnp.bfloat16


def _params(n_axes, vmem=VMEM_LIMIT):
    return pltpu.CompilerParams(
        dimension_semantics=("arbitrary",) * n_axes, vmem_limit_bytes=vmem)


def _dot(a, b):
    return jnp.dot(a, b, preferred_element_type=F32)


def _rms(x, g):
    ms = jnp.mean(x * x, axis=-1, keepdims=True)
    return x * lax.rsqrt(ms + EPS) * g


def _rmsnorm_body(x_ref, g_ref, o_ref):
    o_ref[...] = _rms(x_ref[...], g_ref[...]).astype(o_ref.dtype)


def _rmsnorm(x, g, tm=512):
    s, d = x.shape
    return pl.pallas_call(
        _rmsnorm_body,
        grid=(s // tm,),
        in_specs=[pl.BlockSpec((tm, d), lambda m: (m, 0)),
                  pl.BlockSpec((1, d), lambda m: (0, 0))],
        out_specs=pl.BlockSpec((tm, d), lambda m: (m, 0)),
        out_shape=jax.ShapeDtypeStruct((s, d), BF16),
        compiler_params=_params(1),
        name="rmsnorm_first",
    )(x, g.reshape(1, d))


def _weight_spec(layer, d, tn, col_block):
    return pl.BlockSpec((None, d, tn), lambda n, m: (layer, 0, col_block(n)))


def _vec_spec(layer, rows, tn, col_block):
    return pl.BlockSpec((None, rows, tn), lambda n, m: (layer, 0, col_block(n)))


def _cast_weights_once(pairs):
    @pl.when(pl.program_id(1) == 0)
    def _():
        for src, dst in pairs:
            dst[...] = src[...].astype(dst.dtype)


def _units(tm, tn):
    return [(slice(r, r + UNIT_ROWS), slice(c, c + MXU_COLS))
            for c in range(0, tn, MXU_COLS) for r in range(0, tm, UNIT_ROWS)]


def _pipeline_units(units, issue, finish):
    issue(units[0], 0)
    for k, unit in enumerate(units):
        if k + 1 < len(units):
            issue(units[k + 1], (k + 1) % 2)
        finish(unit, k % 2)


def _traced_zero():
    return jnp.minimum(pl.program_id(1), 0)


def _raw_rows(raw_ref, base, r0, n):
    return raw_ref[pl.ds(pl.multiple_of(base + r0, SUBLANES), n), :]


def _conv_rows(raw_ref, tail_ref, cs, base, r0, w_ref):
    cur = _raw_rows(raw_ref, base, r0, ROW_CHUNK)
    prev = tail_ref[:, cs] if r0 == 0 else _raw_rows(raw_ref, base, r0 - SUBLANES, SUBLANES)
    ext = jnp.concatenate([prev, cur], axis=0)
    s1 = pltpu.roll(ext, 1, 0)[SUBLANES:]
    s2 = pltpu.roll(ext, 2, 0)[SUBLANES:]
    return w_ref[2:3, cs] * cur + w_ref[1:2, cs] * s1 + w_ref[0:1, cs] * s2


def _reset_tails(tail_refs):
    @pl.when(pl.program_id(1) == 0)
    def _():
        for t in tail_refs:
            t[...] = jnp.zeros(t.shape, t.dtype)


def _raw_slots(n):
    return [pltpu.VMEM((UNIT_ROWS, MXU_COLS), F32)] * n


def _sc_in_body(x_ref, wb_ref, wc_ref, wh_ref, cw_ref, y_ref, wbb_ref, wcb_ref, whb_ref, tail_ref,
                p0_ref, p1_ref, g0_ref, g1_ref):
    _cast_weights_once([(wb_ref, wbb_ref), (wc_ref, wcb_ref), (wh_ref, whb_ref)])
    _reset_tails([tail_ref])
    raw_p, raw_gate = (p0_ref, p1_ref), (g0_ref, g1_ref)

    def issue(unit, slot):
        rows, cs = unit
        x = x_ref[rows, :]
        raw_p[slot][...] = _dot(x, wcb_ref[:, cs]) * _dot(x, whb_ref[:, cs])
        raw_gate[slot][...] = _dot(x, wbb_ref[:, cs])

    base = _traced_zero()

    def finish(unit, slot):
        rows, cs = unit
        for r0 in range(0, UNIT_ROWS, ROW_CHUNK):
            conv = _conv_rows(raw_p[slot], tail_ref, cs, base, r0, cw_ref)
            gate = _raw_rows(raw_gate[slot], base, r0, ROW_CHUNK)
            y_ref[rows.start + r0:rows.start + r0 + ROW_CHUNK, cs] = (gate * conv).astype(y_ref.dtype)
        tail_ref[:, cs] = _raw_rows(raw_p[slot], base, UNIT_ROWS - SUBLANES, SUBLANES)

    _pipeline_units(_units(*y_ref.shape), issue, finish)


def _sc_in(xn, w_in, conv_w, layer, tm=1024, tn=512):
    s, d = xn.shape
    c = conv_w.shape[-1]
    nb = c // tn
    part = lambda k: (lambda n: n + k * nb)
    return pl.pallas_call(
        _sc_in_body,
        grid=(nb, s // tm),
        in_specs=[pl.BlockSpec((tm, d), lambda n, m: (m, 0)),
                  _weight_spec(layer, d, tn, part(0)),
                  _weight_spec(layer, d, tn, part(1)),
                  _weight_spec(layer, d, tn, part(2)),
                  _vec_spec(layer, CONV_TAPS, tn, part(0))],
        out_specs=pl.BlockSpec((tm, tn), lambda n, m: (m, n)),
        out_shape=jax.ShapeDtypeStruct((s, c), BF16),
        scratch_shapes=[pltpu.VMEM((d, tn), BF16)] * 3 + [pltpu.VMEM((SUBLANES, tn), F32)] + _raw_slots(4),
        compiler_params=_params(2),
        name="sc_in",
    )(xn, w_in, w_in, w_in, conv_w)


def _ffn_up_body(x_ref, wg_ref, wv_ref, cwg_ref, cwv_ref, bg_ref, bv_ref, a_ref,
                 wgb_ref, wvb_ref, tailg_ref, tailv_ref, g0_ref, g1_ref, v0_ref, v1_ref):
    _cast_weights_once([(wg_ref, wgb_ref), (wv_ref, wvb_ref)])
    _reset_tails([tailg_ref, tailv_ref])
    raw_g, raw_v = (g0_ref, g1_ref), (v0_ref, v1_ref)

    def issue(unit, slot):
        rows, cs = unit
        x = x_ref[rows, :]
        raw_g[slot][...] = _dot(x, wgb_ref[:, cs])
        raw_v[slot][...] = _dot(x, wvb_ref[:, cs])

    base = _traced_zero()

    def finish(unit, slot):
        rows, cs = unit
        for r0 in range(0, UNIT_ROWS, ROW_CHUNK):
            cg = _conv_rows(raw_g[slot], tailg_ref, cs, base, r0, cwg_ref) + bg_ref[:, cs]
            cv = _conv_rows(raw_v[slot], tailv_ref, cs, base, r0, cwv_ref) + bv_ref[:, cs]
            a_ref[rows.start + r0:rows.start + r0 + ROW_CHUNK, cs] = (
                cg * jax.nn.sigmoid(cg) * cv).astype(a_ref.dtype)
        tailg_ref[:, cs] = _raw_rows(raw_g[slot], base, UNIT_ROWS - SUBLANES, SUBLANES)
        tailv_ref[:, cs] = _raw_rows(raw_v[slot], base, UNIT_ROWS - SUBLANES, SUBLANES)

    _pipeline_units(_units(*a_ref.shape), issue, finish)


def _ffn_up(xn, w_up, conv_w, conv_b, layer, tm=2048, tn=512):
    s, d = xn.shape
    f = w_up.shape[-1] // 2
    nb = f // tn
    part = lambda k: (lambda n: n + k * nb)
    conv_b = conv_b.reshape(conv_b.shape[0], 1, 2 * f)
    return pl.pallas_call(
        _ffn_up_body,
        grid=(nb, s // tm),
        in_specs=[pl.BlockSpec((tm, d), lambda n, m: (m, 0)),
                  _weight_spec(layer, d, tn, part(0)), _weight_spec(layer, d, tn, part(1)),
                  _vec_spec(layer, CONV_TAPS, tn, part(0)), _vec_spec(layer, CONV_TAPS, tn, part(1)),
                  _vec_spec(layer, 1, tn, part(0)), _vec_spec(layer, 1, tn, part(1))],
        out_specs=pl.BlockSpec((tm, tn), lambda n, m: (m, n)),
        out_shape=jax.ShapeDtypeStruct((s, f), BF16),
        scratch_shapes=([pltpu.VMEM((d, tn), BF16)] * 2 + [pltpu.VMEM((SUBLANES, tn), F32)] * 2
                        + _raw_slots(4)),
        compiler_params=_params(2),
        name="ffn_up",
    )(xn, w_up, w_up, conv_w, conv_w, conv_b, conv_b)


def _proj_body(x_ref, w_ref, o_ref, wb_ref, r0_ref, r1_ref, *, epilogue, scaled_blocks, scale):
    _cast_weights_once([(w_ref, wb_ref)])
    raw = (r0_ref, r1_ref)
    if epilogue == "qscale":
        factor = jnp.where(pl.program_id(0) < scaled_blocks, scale, 1.0).astype(F32)

    def issue(unit, slot):
        rows, cs = unit
        raw[slot][...] = _dot(x_ref[rows, :], wb_ref[:, cs])

    base = _traced_zero()

    def finish(unit, slot):
        rows, cs = unit
        for r0 in range(0, UNIT_ROWS, ROW_CHUNK):
            acc = _raw_rows(raw[slot], base, r0, ROW_CHUNK)
            if epilogue == "gelu":
                acc = 0.5 * acc * (1.0 + lax.erf(acc * (1.0 / math.sqrt(2.0))))
            elif epilogue == "qscale":
                acc = acc * factor
            o_ref[rows.start + r0:rows.start + r0 + ROW_CHUNK, cs] = acc.astype(o_ref.dtype)

    _pipeline_units(_units(*o_ref.shape), issue, finish)


def _proj(xn, w, layer, epilogue, *, scaled_cols=0, scale=1.0, tm=2048, tn=1024, name):
    s, d = xn.shape
    n = w.shape[-1]
    body = functools.partial(_proj_body, epilogue=epilogue, scaled_blocks=scaled_cols // tn, scale=scale)
    return pl.pallas_call(
        body,
        grid=(n // tn, s // tm),
        in_specs=[pl.BlockSpec((tm, d), lambda j, m: (m, 0)),
                  _weight_spec(layer, d, tn, lambda j: j)],
        out_specs=pl.BlockSpec((tm, tn), lambda j, m: (m, j)),
        out_shape=jax.ShapeDtypeStruct((s, n), BF16),
        scratch_shapes=[pltpu.VMEM((d, tn), BF16)] + _raw_slots(2),
        compiler_params=_params(2),
        name=name,
    )(xn, w)


WEIGHT_STAGE_ROWS = 256


def _stage_weights(w_hbm, layer, wb_ref, stage_ref, sem):
    n = wb_ref.shape[0] // WEIGHT_STAGE_ROWS

    def copy(c):
        return pltpu.make_async_copy(
            w_hbm.at[layer, pl.ds(c * WEIGHT_STAGE_ROWS, WEIGHT_STAGE_ROWS), :],
            stage_ref.at[c % 2], sem.at[c % 2])

    copy(0).start()
    for c in range(n):
        if c + 1 < n:
            copy(c + 1).start()
        copy(c).wait()
        wb_ref[c * WEIGHT_STAGE_ROWS:(c + 1) * WEIGHT_STAGE_ROWS, :] = stage_ref[c % 2].astype(wb_ref.dtype)


def _out_body(a_ref, w_hbm, h_ref, gp_ref, *rest, layer, emit_next):
    if emit_next:
        gn_ref, ho_ref, xn_ref, wb_ref, stage_ref, sem = rest
    else:
        ho_ref, wb_ref, stage_ref, sem = rest

    @pl.when(pl.program_id(0) == 0)
    def _():
        _stage_weights(w_hbm, layer, wb_ref, stage_ref, sem)

    h = h_ref[...] + _rms(_dot(a_ref[...], wb_ref[...]), gp_ref[...])
    ho_ref[...] = h
    if emit_next:
        xn_ref[...] = _rms(h, gn_ref[...]).astype(xn_ref.dtype)


def _out_proj(a, w, layer, h, g_post, g_next, tm, name):
    s, k = a.shape
    d = w.shape[-1]
    emit_next = g_next is not None
    row = lambda width: pl.BlockSpec((tm, width), lambda i: (i, 0))
    vec = pl.BlockSpec((1, d), lambda i: (0, 0))
    in_specs = [row(k), pl.BlockSpec(memory_space=pl.ANY), row(d), vec]
    args = [a, w, h, g_post.reshape(1, d)]
    out_specs, out_shape = [row(d)], [jax.ShapeDtypeStruct((s, d), F32)]
    if emit_next:
        in_specs.append(vec)
        args.append(g_next.reshape(1, d))
        out_specs.append(row(d))
        out_shape.append(jax.ShapeDtypeStruct((s, d), BF16))
    outs = pl.pallas_call(
        functools.partial(_out_body, layer=layer, emit_next=emit_next),
        grid=(s // tm,),
        in_specs=in_specs,
        out_specs=out_specs,
        out_shape=out_shape,
        scratch_shapes=[pltpu.VMEM((k, d), BF16),
                        pltpu.VMEM((2, WEIGHT_STAGE_ROWS, d), F32),
                        pltpu.SemaphoreType.DMA((2,))],
        compiler_params=_params(1),
        name=name if emit_next else name + "_last",
    )(*args)
    return (outs[0], outs[1]) if emit_next else (outs[0], None)


def _sgu_body(u_ref, v_ref, g_ref, b_ref, ws_ref, bias_ref, y_ref):
    v = v_ref[...].astype(F32)
    mu = jnp.mean(v, axis=-1, keepdims=True)
    vc = v - mu
    var = jnp.mean(vc * vc, axis=-1, keepdims=True)
    vn = (vc * lax.rsqrt(var + EPS) * g_ref[...] + b_ref[...]).astype(BF16)
    t = SGU_CHUNK
    gw = v.shape[1] // SGU_GROUPS
    tril = (lax.broadcasted_iota(jnp.int32, (t, t), 0) >= lax.broadcasted_iota(jnp.int32, (t, t), 1))
    for g in range(SGU_GROUPS):
        ws = jnp.where(tril, ws_ref[g], 0.0).astype(BF16)
        cols = slice(g * gw, (g + 1) * gw)
        for c in range(v.shape[0] // t):
            rows = slice(c * t, (c + 1) * t)
            mixed = _dot(ws, vn[rows, cols]) + bias_ref[:, cols]
            y_ref[rows, cols] = (u_ref[rows, cols].astype(F32) * mixed).astype(y_ref.dtype)


def _sgu(h, ln_g, ln_b, w_s, bias_full, tm=256):
    s, two_d = h.shape
    d = two_d // 2
    vec = pl.BlockSpec((1, d), lambda m: (0, 0))
    return pl.pallas_call(
        _sgu_body,
        grid=(s // tm,),
        in_specs=[pl.BlockSpec((tm, d), lambda m: (m, 0)),
                  pl.BlockSpec((tm, d), lambda m: (m, 1)),
                  vec, vec,
                  pl.BlockSpec(w_s.shape, lambda m: (0, 0, 0)),
                  pl.BlockSpec(bias_full.shape, lambda m: (0, 0))],
        out_specs=pl.BlockSpec((tm, d), lambda m: (m, 0)),
        out_shape=jax.ShapeDtypeStruct((s, d), BF16),
        compiler_params=_params(1),
        name="sgu_mix",
    )(h, h, ln_g.reshape(1, d), ln_b.reshape(1, d), w_s, bias_full)


class _SbVisit:
    def __init__(self, q_ref, k_ref, v_ref, uo, hd, j, mask):
        t = SB_TILE
        self.cols = slice(hd * SB_HEAD_DIM, (hd + 1) * SB_HEAD_DIM)
        self.rows = pl.ds(pl.multiple_of(j * t, t), t)
        self.q_ref, self.k_ref, self.v_ref, self.uo, self.hd, self.mask = q_ref, k_ref, v_ref, uo, hd, mask

    def scores(self):
        q = self.q_ref[:, self.cols]
        kb = self.k_ref[self.rows, self.cols]
        self.z = lax.dot_general(q, kb, (((1,), (1,)), ((), ())), preferred_element_type=F32)

    def log_terms(self):
        z = self.z
        self.zl = jnp.minimum(z, 0.0) - jnp.log(1.0 + jnp.exp(-jnp.abs(z)))
        l = self.zl - z
        if self.mask is not None:
            l = jnp.where(self.mask, l, 0.0)
        self.l = l.astype(BF16)

    def tile_sums(self):
        self.sums = _dot(self.l, self.uo)

    def weights(self, carry):
        t = SB_TILE
        a = jnp.exp(self.zl + self.sums[:, :t] + jnp.tile(carry, (1, t // LANES)))
        if self.mask is not None:
            a = jnp.where(self.mask, a, 0.0)
        self.a = a.astype(BF16)
        return carry + self.sums[:, t:]

    def values(self, acc):
        return acc + _dot(self.a, self.v_ref[self.rows, self.cols])

    STAGES = 5

    def run_stage(self, s, acc, carry):
        if s == 0:
            self.scores()
        elif s == 1:
            self.log_terms()
        elif s == 2:
            self.tile_sums()
        elif s == 3:
            carry[self.hd] = self.weights(carry[self.hd])
        else:
            acc[self.hd] = self.values(acc[self.hd])


def _sb_pipeline(visits, acc, carry):
    for step in range(len(visits) + _SbVisit.STAGES - 1):
        for s in reversed(range(_SbVisit.STAGES)):
            v = step - s
            if 0 <= v < len(visits):
                visits[v].run_stage(s, acc, carry)


def _sb_attn_body(q_ref, k_ref, v_ref, o_ref, acc_ref, carry_ref):
    t = SB_TILE
    i = pl.program_id(1)
    r = lax.broadcasted_iota(jnp.int32, (t, t), 0)
    c = lax.broadcasted_iota(jnp.int32, (t, t), 1)
    causal = c < r
    ur = lax.broadcasted_iota(jnp.int32, (t, t + LANES), 0)
    uc = lax.broadcasted_iota(jnp.int32, (t, t + LANES), 1)
    uo = jnp.where((ur > uc) | (uc >= t), 1.0, 0.0).astype(BF16)
    heads = range(SB_HEAD_GROUP)
    zeros = jnp.zeros((t, SB_HEAD_DIM), F32)

    def visit_tiles(tiles, acc, carry):
        visits = [_SbVisit(q_ref, k_ref, v_ref, uo, hd, j, mask) for j, mask in tiles for hd in heads]
        _sb_pipeline(visits, acc, carry)
        for hd in heads:
            acc_ref[hd], carry_ref[hd] = acc[hd], carry[hd]

    def fresh():
        return {hd: zeros for hd in heads}, {hd: zeros for hd in heads}

    windowed = i >= SB_WINDOW - 1

    @pl.when(windowed)
    def _():
        visit_tiles([(i - w, causal if w == 0 else None) for w in range(SB_WINDOW)], *fresh())

    @pl.when(jnp.logical_not(windowed))
    def _():
        visit_tiles([(i, causal)], *fresh())

    def cond(state):
        j, cmax = state
        return jnp.logical_and(j >= 0, cmax > -SB_EXP_ZERO)

    def body(state):
        j, _ = state
        visit_tiles([(j, None)], {hd: acc_ref[hd] for hd in heads}, {hd: carry_ref[hd] for hd in heads})
        return j - 1, jnp.max(carry_ref[...])

    first = jnp.where(windowed, i - SB_WINDOW, i - 1)
    lax.while_loop(cond, body, (first, jnp.max(carry_ref[...])))
    for hd in heads:
        o_ref[:, hd * SB_HEAD_DIM:(hd + 1) * SB_HEAD_DIM] = acc_ref[hd].astype(o_ref.dtype)


def _sb_attn(qkv):
    s = qkv.shape[0]
    t = SB_TILE
    gw = SB_HEAD_GROUP * SB_HEAD_DIM
    ng = SB_HEADS // SB_HEAD_GROUP
    kv_spec = lambda part: pl.BlockSpec((s, gw), lambda g, i: (0, part * ng + g),
                                        pipeline_mode=pl.Buffered(1))
    return pl.pallas_call(
        _sb_attn_body,
        grid=(ng, s // t),
        in_specs=[pl.BlockSpec((t, gw), lambda g, i: (i, g)), kv_spec(1), kv_spec(2)],
        out_specs=pl.BlockSpec((t, gw), lambda g, i: (i, g)),
        out_shape=jax.ShapeDtypeStruct((s, SB_HEADS * SB_HEAD_DIM), BF16),
        scratch_shapes=[pltpu.VMEM((SB_HEAD_GROUP, t, SB_HEAD_DIM), F32),
                        pltpu.VMEM((SB_HEAD_GROUP, t, LANES), F32)],
        compiler_params=_params(2),
        name="sb_attn",
    )(qkv, qkv, qkv)


def kernel(x, norm_mix_pre, norm_mix_post, norm_ffn_pre, norm_ffn_post, sc_w_in, sc_conv_w, sc_w_out, sg_w_in, sg_ln_g, sg_ln_b, sg_w_s, sg_b_s, sg_w_out, sb_w_qkv, sb_w_out, ffn_w_up, ffn_conv_w, ffn_conv_b, ffn_w_down):
    batch, seq, d = x.shape
    assert batch == 1, "row tiles carry the causal conv state across the flattened sequence"
    depth = norm_mix_pre.shape[0]
    n_mixers = 3

    h = x.reshape(seq, d)
    xn = _rmsnorm(h, norm_mix_pre[0])
    for i in range(depth):
        kind, j = i % n_mixers, i // n_mixers
        if kind == 0:
            a = _sc_in(xn, sc_w_in, sc_conv_w, j)
            w_out = sc_w_out
        elif kind == 1:
            hid = _proj(xn, sg_w_in, j, "gelu", name="sg_in")
            bias_full = jnp.repeat(sg_b_s[j].T, hid.shape[1] // 2 // SGU_GROUPS, axis=1)
            a = _sgu(hid, sg_ln_g[j], sg_ln_b[j], sg_w_s[j], bias_full)
            w_out = sg_w_out
        else:
            qkv = _proj(xn, sb_w_qkv, j, "qscale", scaled_cols=SB_HEADS * SB_HEAD_DIM,
                        scale=SB_HEAD_DIM ** -0.5, name="sb_qkv")
            a = _sb_attn(qkv)
            w_out = sb_w_out
        h, xn = _out_proj(a, w_out, j, h, norm_mix_post[i], norm_ffn_pre[i], 512, "mix_out")
        act = _ffn_up(xn, ffn_w_up, ffn_conv_w, ffn_conv_b, i)
        g_next = norm_mix_pre[i + 1] if i + 1 < depth else None
        h, xn = _out_proj(act, ffn_w_down, i, h, norm_ffn_post[i], g_next, 256, "ffn_down")
    return h.reshape(batch, seq, d)
```

```python
import functools
import math

import jax
import jax.numpy as jnp
from jax import lax
from jax.experimental import pallas as pl
from jax.experimental.pallas import tpu as pltpu

EPS = 1e-6
CONV_TAPS = 3
SUBLANES = 8
LANES = 128
MXU_COLS = 256
UNIT_ROWS = 256
ROW_CHUNK = 64
SGU_CHUNK = 128
SGU_GROUPS = 8
SB_HEADS = 16
SB_HEAD_DIM = 128
SB_TILE = 256
SB_HEAD_GROUP = 16
SB_EXP_ZERO = 106.0
MIB = 1024 * 1024
VMEM_LIMIT = 56 * MIB

F32 = jnp.float32
BF16 = jnp.bfloat16


def _params(n_axes, vmem=VMEM_LIMIT):
    return pltpu.CompilerParams(
        dimension_semantics=("arbitrary",) * n_axes, vmem_limit_bytes=vmem)


def _dot(a, b):
    return jnp.dot(a, b, preferred_element_type=F32)


def _rms(x, g):
    ms = jnp.mean(x * x, axis=-1, keepdims=True)
    return x * lax.rsqrt(ms + EPS) * g


def _rmsnorm_body(x_ref, g_ref, o_ref):
    o_ref[...] = _rms(x_ref[...], g_ref[...]).astype(o_ref.dtype)


def _rmsnorm(x, g, tm=512):
    s, d = x.shape
    return pl.pallas_call(
        _rmsnorm_body,
        grid=(s // tm,),
        in_specs=[pl.BlockSpec((tm, d), lambda m: (m, 0)),
                  pl.BlockSpec((1, d), lambda m: (0, 0))],
        out_specs=pl.BlockSpec((tm, d), lambda m: (m, 0)),
        out_shape=jax.ShapeDtypeStruct((s, d), BF16),
        compiler_params=_params(1),
        name="rmsnorm_first",
    )(x, g.reshape(1, d))


def _weight_spec(layer, d, tn, col_block):
    return pl.BlockSpec((None, d, tn), lambda n, m: (layer, 0, col_block(n)))


def _vec_spec(layer, rows, tn, col_block):
    return pl.BlockSpec((None, rows, tn), lambda n, m: (layer, 0, col_block(n)))


def _cast_weights_once(pairs):
    @pl.when(pl.program_id(1) == 0)
    def _():
        for src, dst in pairs:
            dst[...] = src[...].astype(dst.dtype)


def _units(tm, tn):
    return [(slice(r, r + UNIT_ROWS), slice(c, c + MXU_COLS))
            for c in range(0, tn, MXU_COLS) for r in range(0, tm, UNIT_ROWS)]


def _pipeline_units(units, issue, finish):
    issue(units[0], 0)
    for k, unit in enumerate(units):
        if k + 1 < len(units):
            issue(units[k + 1], (k + 1) % 2)
        finish(unit, k % 2)


def _traced_zero():
    return jnp.minimum(pl.program_id(1), 0)


def _raw_rows(raw_ref, base, r0, n):
    return raw_ref[pl.ds(pl.multiple_of(base + r0, SUBLANES), n), :]


def _conv_rows(raw_ref, tail_ref, cs, base, r0, w_ref):
    cur = _raw_rows(raw_ref, base, r0, ROW_CHUNK)
    prev = tail_ref[:, cs] if r0 == 0 else _raw_rows(raw_ref, base, r0 - SUBLANES, SUBLANES)
    ext = jnp.concatenate([prev, cur], axis=0)
    s1 = pltpu.roll(ext, 1, 0)[SUBLANES:]
    s2 = pltpu.roll(ext, 2, 0)[SUBLANES:]
    return w_ref[2:3, cs] * cur + w_ref[1:2, cs] * s1 + w_ref[0:1, cs] * s2


def _reset_tails(tail_refs):
    @pl.when(pl.program_id(1) == 0)
    def _():
        for t in tail_refs:
            t[...] = jnp.zeros(t.shape, t.dtype)


def _raw_slots(n):
    return [pltpu.VMEM((UNIT_ROWS, MXU_COLS), F32)] * n


def _sc_in_body(x_ref, wb_ref, wc_ref, wh_ref, cw_ref, y_ref, wbb_ref, wcb_ref, whb_ref, tail_ref,
                p0_ref, p1_ref, g0_ref, g1_ref):
    _cast_weights_once([(wb_ref, wbb_ref), (wc_ref, wcb_ref), (wh_ref, whb_ref)])
    _reset_tails([tail_ref])
    raw_p, raw_gate = (p0_ref, p1_ref), (g0_ref, g1_ref)

    def issue(unit, slot):
        rows, cs = unit
        n = rows.stop - rows.start
        x = x_ref[rows, :]
        raw_p[slot][0:n, :] = _dot(x, wcb_ref[:, cs]) * _dot(x, whb_ref[:, cs])
        raw_gate[slot][0:n, :] = _dot(x, wbb_ref[:, cs])

    base = _traced_zero()

    def finish(unit, slot):
        rows, cs = unit
        n = rows.stop - rows.start
        for r0 in range(0, n, ROW_CHUNK):
            conv = _conv_rows(raw_p[slot], tail_ref, cs, base, r0, cw_ref)
            gate = _raw_rows(raw_gate[slot], base, r0, ROW_CHUNK)
            y_ref[rows.start + r0:rows.start + r0 + ROW_CHUNK, cs] = (gate * conv).astype(y_ref.dtype)
        tail_ref[:, cs] = _raw_rows(raw_p[slot], base, n - SUBLANES, SUBLANES)

    _pipeline_units(_units(*y_ref.shape), issue, finish)


def _sc_in(xn, w_in, conv_w, layer, tm=2048, tn=512):
    s, d = xn.shape
    c = conv_w.shape[-1]
    nb = c // tn
    part = lambda k: (lambda n: n + k * nb)
    return pl.pallas_call(
        _sc_in_body,
        grid=(nb, s // tm),
        in_specs=[pl.BlockSpec((tm, d), lambda n, m: (m, 0)),
                  _weight_spec(layer, d, tn, part(0)),
                  _weight_spec(layer, d, tn, part(1)),
                  _weight_spec(layer, d, tn, part(2)),
                  _vec_spec(layer, CONV_TAPS, tn, part(0))],
        out_specs=pl.BlockSpec((tm, tn), lambda n, m: (m, n)),
        out_shape=jax.ShapeDtypeStruct((s, c), BF16),
        scratch_shapes=[pltpu.VMEM((d, tn), BF16)] * 3 + [pltpu.VMEM((SUBLANES, tn), F32)] + _raw_slots(4),
        compiler_params=_params(2),
        name="sc_in",
    )(xn, w_in, w_in, w_in, conv_w)


def _ffn_up_body(x_ref, wg_ref, wv_ref, cwg_ref, cwv_ref, bg_ref, bv_ref, a_ref,
                 wgb_ref, wvb_ref, tailg_ref, tailv_ref, g0_ref, g1_ref, v0_ref, v1_ref):
    _cast_weights_once([(wg_ref, wgb_ref), (wv_ref, wvb_ref)])
    _reset_tails([tailg_ref, tailv_ref])
    raw_g, raw_v = (g0_ref, g1_ref), (v0_ref, v1_ref)

    def issue(unit, slot):
        rows, cs = unit
        n = rows.stop - rows.start
        x = x_ref[rows, :]
        raw_g[slot][0:n, :] = _dot(x, wgb_ref[:, cs])
        raw_v[slot][0:n, :] = _dot(x, wvb_ref[:, cs])

    base = _traced_zero()

    def finish(unit, slot):
        rows, cs = unit
        n = rows.stop - rows.start
        for r0 in range(0, n, ROW_CHUNK):
            cg = _conv_rows(raw_g[slot], tailg_ref, cs, base, r0, cwg_ref) + bg_ref[:, cs]
            cv = _conv_rows(raw_v[slot], tailv_ref, cs, base, r0, cwv_ref) + bv_ref[:, cs]
            a_ref[rows.start + r0:rows.start + r0 + ROW_CHUNK, cs] = (
                cg * jax.nn.sigmoid(cg) * cv).astype(a_ref.dtype)
        tailg_ref[:, cs] = _raw_rows(raw_g[slot], base, n - SUBLANES, SUBLANES)
        tailv_ref[:, cs] = _raw_rows(raw_v[slot], base, n - SUBLANES, SUBLANES)

    _pipeline_units(_units(*a_ref.shape), issue, finish)


def _ffn_up(xn, w_up, conv_w, conv_b, layer, tm=2048, tn=512):
    s, d = xn.shape
    f = w_up.shape[-1] // 2
    nb = f // tn
    part = lambda k: (lambda n: n + k * nb)
    conv_b = conv_b.reshape(conv_b.shape[0], 1, 2 * f)
    return pl.pallas_call(
        _ffn_up_body,
        grid=(nb, s // tm),
        in_specs=[pl.BlockSpec((tm, d), lambda n, m: (m, 0)),
                  _weight_spec(layer, d, tn, part(0)), _weight_spec(layer, d, tn, part(1)),
                  _vec_spec(layer, CONV_TAPS, tn, part(0)), _vec_spec(layer, CONV_TAPS, tn, part(1)),
                  _vec_spec(layer, 1, tn, part(0)), _vec_spec(layer, 1, tn, part(1))],
        out_specs=pl.BlockSpec((tm, tn), lambda n, m: (m, n)),
        out_shape=jax.ShapeDtypeStruct((s, f), BF16),
        scratch_shapes=([pltpu.VMEM((d, tn), BF16)] * 2 + [pltpu.VMEM((SUBLANES, tn), F32)] * 2
                        + _raw_slots(4)),
        compiler_params=_params(2),
        name="ffn_up",
    )(xn, w_up, w_up, conv_w, conv_w, conv_b, conv_b)


def _proj_body(x_ref, w_ref, o_ref, wb_ref, r0_ref, r1_ref, *, epilogue, scaled_blocks, scale):
    _cast_weights_once([(w_ref, wb_ref)])
    raw = (r0_ref, r1_ref)
    if epilogue == "qscale":
        factor = jnp.where(pl.program_id(0) < scaled_blocks, scale, 1.0).astype(F32)

    def issue(unit, slot):
        rows, cs = unit
        raw[slot][0:rows.stop - rows.start, :] = _dot(x_ref[rows, :], wb_ref[:, cs])

    base = _traced_zero()

    def finish(unit, slot):
        rows, cs = unit
        for r0 in range(0, rows.stop - rows.start, ROW_CHUNK):
            acc = _raw_rows(raw[slot], base, r0, ROW_CHUNK)
            if epilogue == "gelu":
                acc = 0.5 * acc * (1.0 + lax.erf(acc * (1.0 / math.sqrt(2.0))))
            elif epilogue == "qscale":
                acc = acc * factor
            o_ref[rows.start + r0:rows.start + r0 + ROW_CHUNK, cs] = acc.astype(o_ref.dtype)

    _pipeline_units(_units(*o_ref.shape), issue, finish)


def _proj(xn, w, layer, epilogue, *, scaled_cols=0, scale=1.0, tm=2048, tn=1024, name):
    s, d = xn.shape
    n = w.shape[-1]
    body = functools.partial(_proj_body, epilogue=epilogue, scaled_blocks=scaled_cols // tn, scale=scale)
    return pl.pallas_call(
        body,
        grid=(n // tn, s // tm),
        in_specs=[pl.BlockSpec((tm, d), lambda j, m: (m, 0)),
                  _weight_spec(layer, d, tn, lambda j: j)],
        out_specs=pl.BlockSpec((tm, tn), lambda j, m: (m, j)),
        out_shape=jax.ShapeDtypeStruct((s, n), BF16),
        scratch_shapes=[pltpu.VMEM((d, tn), BF16)] + _raw_slots(2),
        compiler_params=_params(2),
        name=name,
    )(xn, w)


WEIGHT_STAGE_ROWS = 256


def _stage_weights(w_hbm, layer, wb_ref, stage_ref, sem):
    n = wb_ref.shape[0] // WEIGHT_STAGE_ROWS

    def copy(c):
        return pltpu.make_async_copy(
            w_hbm.at[layer, pl.ds(c * WEIGHT_STAGE_ROWS, WEIGHT_STAGE_ROWS), :],
            stage_ref.at[c % 2], sem.at[c % 2])

    copy(0).start()
    for c in range(n):
        if c + 1 < n:
            copy(c + 1).start()
        copy(c).wait()
        wb_ref[c * WEIGHT_STAGE_ROWS:(c + 1) * WEIGHT_STAGE_ROWS, :] = stage_ref[c % 2].astype(wb_ref.dtype)


def _out_body(a_ref, w_hbm, h_ref, gp_ref, *rest, layer, emit_next):
    if emit_next:
        gn_ref, ho_ref, xn_ref, wb_ref, stage_ref, sem = rest
    else:
        ho_ref, wb_ref, stage_ref, sem = rest

    @pl.when(pl.program_id(0) == 0)
    def _():
        _stage_weights(w_hbm, layer, wb_ref, stage_ref, sem)

    h = h_ref[...] + _rms(_dot(a_ref[...], wb_ref[...]), gp_ref[...])
    ho_ref[...] = h
    if emit_next:
        xn_ref[...] = _rms(h, gn_ref[...]).astype(xn_ref.dtype)


def _out_proj(a, w, layer, h, g_post, g_next, tm, name):
    s, k = a.shape
    d = w.shape[-1]
    emit_next = g_next is not None
    row = lambda width: pl.BlockSpec((tm, width), lambda i: (i, 0))
    vec = pl.BlockSpec((1, d), lambda i: (0, 0))
    in_specs = [row(k), pl.BlockSpec(memory_space=pl.ANY), row(d), vec]
    args = [a, w, h, g_post.reshape(1, d)]
    out_specs, out_shape = [row(d)], [jax.ShapeDtypeStruct((s, d), F32)]
    if emit_next:
        in_specs.append(vec)
        args.append(g_next.reshape(1, d))
        out_specs.append(row(d))
        out_shape.append(jax.ShapeDtypeStruct((s, d), BF16))
    outs = pl.pallas_call(
        functools.partial(_out_body, layer=layer, emit_next=emit_next),
        grid=(s // tm,),
        in_specs=in_specs,
        out_specs=out_specs,
        out_shape=out_shape,
        scratch_shapes=[pltpu.VMEM((k, d), BF16),
                        pltpu.VMEM((2, WEIGHT_STAGE_ROWS, d), F32),
                        pltpu.SemaphoreType.DMA((2,))],
        compiler_params=_params(1),
        name=name if emit_next else name + "_last",
    )(*args)
    return (outs[0], outs[1]) if emit_next else (outs[0], None)


def _sgu_body(u_ref, v_ref, g_ref, b_ref, ws_ref, bias_ref, y_ref):
    v = v_ref[...].astype(F32)
    mu = jnp.mean(v, axis=-1, keepdims=True)
    vc = v - mu
    var = jnp.mean(vc * vc, axis=-1, keepdims=True)
    vn = (vc * lax.rsqrt(var + EPS) * g_ref[...] + b_ref[...]).astype(BF16)
    t = SGU_CHUNK
    gw = v.shape[1] // SGU_GROUPS
    tril = (lax.broadcasted_iota(jnp.int32, (t, t), 0) >= lax.broadcasted_iota(jnp.int32, (t, t), 1))
    for g in range(SGU_GROUPS):
        ws = jnp.where(tril, ws_ref[g], 0.0).astype(BF16)
        cols = slice(g * gw, (g + 1) * gw)
        for c in range(v.shape[0] // t):
            rows = slice(c * t, (c + 1) * t)
            mixed = _dot(ws, vn[rows, cols]) + bias_ref[:, cols]
            y_ref[rows, cols] = (u_ref[rows, cols].astype(F32) * mixed).astype(y_ref.dtype)


def _sgu(h, ln_g, ln_b, w_s, bias_full, tm=512):
    s, two_d = h.shape
    d = two_d // 2
    vec = pl.BlockSpec((1, d), lambda m: (0, 0))
    return pl.pallas_call(
        _sgu_body,
        grid=(s // tm,),
        in_specs=[pl.BlockSpec((tm, d), lambda m: (m, 0)),
                  pl.BlockSpec((tm, d), lambda m: (m, 1)),
                  vec, vec,
                  pl.BlockSpec(w_s.shape, lambda m: (0, 0, 0)),
                  pl.BlockSpec(bias_full.shape, lambda m: (0, 0))],
        out_specs=pl.BlockSpec((tm, d), lambda m: (m, 0)),
        out_shape=jax.ShapeDtypeStruct((s, d), BF16),
        compiler_params=_params(1),
        name="sgu_mix",
    )(h, h, ln_g.reshape(1, d), ln_b.reshape(1, d), w_s, bias_full)


class _SbVisit:
    def __init__(self, q_ref, k_ref, v_ref, uo, hd, mask):
        self.cols = slice(hd * SB_HEAD_DIM, (hd + 1) * SB_HEAD_DIM)
        self.q_ref, self.k_ref, self.v_ref, self.uo, self.hd, self.mask = q_ref, k_ref, v_ref, uo, hd, mask

    def scores(self):
        q = self.q_ref[:, self.cols]
        kb = self.k_ref[:, self.cols]
        self.z = lax.dot_general(q, kb, (((1,), (1,)), ((), ())), preferred_element_type=F32)

    def log_terms(self):
        z = self.z
        self.zl = jnp.minimum(z, 0.0) - jnp.log(1.0 + jnp.exp(-jnp.abs(z)))
        l = self.zl - z
        if self.mask is not None:
            l = jnp.where(self.mask, l, 0.0)
        self.l = l.astype(BF16)

    def tile_sums(self):
        self.sums = _dot(self.l, self.uo)

    def weights(self, carry):
        t = SB_TILE
        a = jnp.exp(self.zl + self.sums[:, :t] + jnp.tile(carry, (1, t // LANES)))
        if self.mask is not None:
            a = jnp.where(self.mask, a, 0.0)
        self.a = a.astype(BF16)
        return carry + self.sums[:, t:]

    def values(self, acc):
        return acc + _dot(self.a, self.v_ref[:, self.cols])

    STAGES = 5

    def run_stage(self, s, acc, carry):
        if s == 0:
            self.scores()
        elif s == 1:
            self.log_terms()
        elif s == 2:
            self.tile_sums()
        elif s == 3:
            carry[self.hd] = self.weights(carry[self.hd])
        else:
            acc[self.hd] = self.values(acc[self.hd])


def _sb_pipeline(visits, acc, carry):
    for step in range(len(visits) + _SbVisit.STAGES - 1):
        for s in reversed(range(_SbVisit.STAGES)):
            v = step - s
            if 0 <= v < len(visits):
                visits[v].run_stage(s, acc, carry)


def _sb_attn_body(q_ref, kd_ref, kp_ref, vd_ref, vp_ref, qkv_hbm, o_ref,
                  acc_ref, carry_ref, kbuf_ref, vbuf_ref, sem, *, n_groups):
    t = SB_TILE
    g = pl.program_id(0)
    i = pl.program_id(1)
    r = lax.broadcasted_iota(jnp.int32, (t, t), 0)
    c = lax.broadcasted_iota(jnp.int32, (t, t), 1)
    causal = c < r
    ur = lax.broadcasted_iota(jnp.int32, (t, t + LANES), 0)
    uc = lax.broadcasted_iota(jnp.int32, (t, t + LANES), 1)
    uo = jnp.where((ur > uc) | (uc >= t), 1.0, 0.0).astype(BF16)
    heads = range(SB_HEAD_GROUP)
    zeros = jnp.zeros((t, SB_HEAD_DIM), F32)

    def visit_tiles(tiles, acc, carry):
        visits = [_SbVisit(q_ref, k, v, uo, hd, mask) for k, v, mask in tiles for hd in heads]
        _sb_pipeline(visits, acc, carry)
        for hd in heads:
            acc_ref[hd], carry_ref[hd] = acc[hd], carry[hd]

    def fresh():
        return {hd: zeros for hd in heads}, {hd: zeros for hd in heads}

    @pl.when(i > 0)
    def _():
        visit_tiles([(kd_ref, vd_ref, causal), (kp_ref, vp_ref, None)], *fresh())

    @pl.when(i == 0)
    def _():
        visit_tiles([(kd_ref, vd_ref, causal)], *fresh())

    def fetch(j, part, buf_ref, slot):
        gw = buf_ref.shape[1]
        return pltpu.make_async_copy(
            qkv_hbm.at[pl.ds(pl.multiple_of(j * t, t), t),
                       pl.ds(pl.multiple_of((part * n_groups + g) * gw, LANES), gw)],
            buf_ref, sem.at[slot])

    def cond(state):
        j, cmax = state
        return jnp.logical_and(j >= 0, cmax > -SB_EXP_ZERO)

    def body(state):
        j, _ = state
        copies = [fetch(j, 1, kbuf_ref, 0), fetch(j, 2, vbuf_ref, 1)]
        for c in copies:
            c.start()
        for c in copies:
            c.wait()
        visit_tiles([(kbuf_ref, vbuf_ref, None)],
                    {hd: acc_ref[hd] for hd in heads}, {hd: carry_ref[hd] for hd in heads})
        return j - 1, jnp.max(carry_ref[...])

    lax.while_loop(cond, body, (i - 2, jnp.max(carry_ref[...])))
    for hd in heads:
        o_ref[:, hd * SB_HEAD_DIM:(hd + 1) * SB_HEAD_DIM] = acc_ref[hd].astype(o_ref.dtype)


def _sb_attn(qkv):
    s = qkv.shape[0]
    t = SB_TILE
    gw = SB_HEAD_GROUP * SB_HEAD_DIM
    ng = SB_HEADS // SB_HEAD_GROUP
    diag = lambda part: pl.BlockSpec((t, gw), lambda g, i: (i, part * ng + g))
    prev = lambda part: pl.BlockSpec((t, gw), lambda g, i: (jnp.maximum(i - 1, 0), part * ng + g))
    return pl.pallas_call(
        functools.partial(_sb_attn_body, n_groups=ng),
        grid=(ng, s // t),
        in_specs=[diag(0), diag(1), prev(1), diag(2), prev(2), pl.BlockSpec(memory_space=pl.ANY)],
        out_specs=pl.BlockSpec((t, gw), lambda g, i: (i, g)),
        out_shape=jax.ShapeDtypeStruct((s, SB_HEADS * SB_HEAD_DIM), BF16),
        scratch_shapes=[pltpu.VMEM((SB_HEAD_GROUP, t, SB_HEAD_DIM), F32),
                        pltpu.VMEM((SB_HEAD_GROUP, t, LANES), F32),
                        pltpu.VMEM((t, gw), BF16), pltpu.VMEM((t, gw), BF16),
                        pltpu.SemaphoreType.DMA((2,))],
        compiler_params=_params(2),
        name="sb_attn",
    )(qkv, qkv, qkv, qkv, qkv, qkv)


def kernel(x, norm_mix_pre, norm_mix_post, norm_ffn_pre, norm_ffn_post, sc_w_in, sc_conv_w, sc_w_out, sg_w_in, sg_ln_g, sg_ln_b, sg_w_s, sg_b_s, sg_w_out, sb_w_qkv, sb_w_out, ffn_w_up, ffn_conv_w, ffn_conv_b, ffn_w_down):
    batch, seq, d = x.shape
    assert batch == 1, "row tiles carry the causal conv state across the flattened sequence"
    depth = norm_mix_pre.shape[0]
    n_mixers = 3

    h = x.reshape(seq, d)
    xn = _rmsnorm(h, norm_mix_pre[0])
    for i in range(depth):
        kind, j = i % n_mixers, i // n_mixers
        if kind == 0:
            a = _sc_in(xn, sc_w_in, sc_conv_w, j)
            w_out = sc_w_out
        elif kind == 1:
            hid = _proj(xn, sg_w_in, j, "gelu", name="sg_in")
            bias_full = jnp.repeat(sg_b_s[j].T, hid.shape[1] // 2 // SGU_GROUPS, axis=1)
            a = _sgu(hid, sg_ln_g[j], sg_ln_b[j], sg_w_s[j], bias_full)
            w_out = sg_w_out
        else:
            qkv = _proj(xn, sb_w_qkv, j, "qscale", scaled_cols=SB_HEADS * SB_HEAD_DIM,
                        scale=SB_HEAD_DIM ** -0.5, name="sb_qkv")
            a = _sb_attn(qkv)
            w_out = sb_w_out
        h, xn = _out_proj(a, w_out, j, h, norm_mix_post[i], norm_ffn_pre[i], 512, "mix_out")
        act = _ffn_up(xn, ffn_w_up, ffn_conv_w, ffn_conv_b, i)
        g_next = norm_mix_pre[i + 1] if i + 1 < depth else None
        h, xn = _out_proj(act, ffn_w_down, i, h, norm_ffn_post[i], g_next, 256, "ffn_down")
    return h.reshape(batch, seq, d)
```

```python
import functools
import math

import jax
import jax.numpy as jnp
from jax import lax
from jax.experimental import pallas as pl
from jax.experimental.pallas import tpu as pltpu

EPS = 1e-6
CONV_TAPS = 3
SUBLANES = 8
LANES = 128
MXU_COLS = 256
UNIT_ROWS = 256
ROW_CHUNK = 64
SGU_CHUNK = 128
SGU_GROUPS = 8
SB_HEADS = 16
SB_HEAD_DIM = 128
SB_TILE = 256
SB_HEAD_GROUP = 16
SB_EXP_ZERO = 106.0
MIB = 1024 * 1024
VMEM_LIMIT = 56 * MIB

F32 = jnp.float32
BF16 = jnp.bfloat16


def _params(n_axes, vmem=VMEM_LIMIT):
    return pltpu.CompilerParams(
        dimension_semantics=("arbitrary",) * n_axes, vmem_limit_bytes=vmem)


def _dot(a, b):
    return jnp.dot(a, b, preferred_element_type=F32)


def _rms(x, g):
    ms = jnp.mean(x * x, axis=-1, keepdims=True)
    return x * lax.rsqrt(ms + EPS) * g


def _rmsnorm_body(x_ref, g_ref, o_ref):
    o_ref[...] = _rms(x_ref[...], g_ref[...]).astype(o_ref.dtype)


def _rmsnorm(x, g, tm=512):
    s, d = x.shape
    return pl.pallas_call(
        _rmsnorm_body,
        grid=(s // tm,),
        in_specs=[pl.BlockSpec((tm, d), lambda m: (m, 0)),
                  pl.BlockSpec((1, d), lambda m: (0, 0))],
        out_specs=pl.BlockSpec((tm, d), lambda m: (m, 0)),
        out_shape=jax.ShapeDtypeStruct((s, d), BF16),
        compiler_params=_params(1),
        name="rmsnorm_first",
    )(x, g.reshape(1, d))


def _weight_spec(layer, d, tn, col_block):
    return pl.BlockSpec((None, d, tn), lambda n, m: (layer, 0, col_block(n)))


def _vec_spec(layer, rows, tn, col_block):
    return pl.BlockSpec((None, rows, tn), lambda n, m: (layer, 0, col_block(n)))


def _cast_weights_once(pairs):
    @pl.when(pl.program_id(1) == 0)
    def _():
        for src, dst in pairs:
            dst[...] = src[...].astype(dst.dtype)


def _units(tm, tn):
    return [(slice(r, r + UNIT_ROWS), slice(c, c + MXU_COLS))
            for c in range(0, tn, MXU_COLS) for r in range(0, tm, UNIT_ROWS)]


def _pipeline_units(units, issue, finish):
    issue(units[0], 0)
    for k, unit in enumerate(units):
        if k + 1 < len(units):
            issue(units[k + 1], (k + 1) % 2)
        finish(unit, k % 2)


def _traced_zero():
    return jnp.minimum(pl.program_id(1), 0)


def _raw_rows(raw_ref, base, r0, n):
    return raw_ref[pl.ds(pl.multiple_of(base + r0, SUBLANES), n), :]


def _conv_rows(raw_ref, tail_ref, cs, base, r0, w_ref):
    cur = _raw_rows(raw_ref, base, r0, ROW_CHUNK)
    prev = tail_ref[:, cs] if r0 == 0 else _raw_rows(raw_ref, base, r0 - SUBLANES, SUBLANES)
    ext = jnp.concatenate([prev, cur], axis=0)
    s1 = pltpu.roll(ext, 1, 0)[SUBLANES:]
    s2 = pltpu.roll(ext, 2, 0)[SUBLANES:]
    return w_ref[2:3, cs] * cur + w_ref[1:2, cs] * s1 + w_ref[0:1, cs] * s2


def _reset_tails(tail_refs):
    @pl.when(pl.program_id(1) == 0)
    def _():
        for t in tail_refs:
            t[...] = jnp.zeros(t.shape, t.dtype)


def _raw_slots(n):
    return [pltpu.VMEM((UNIT_ROWS, MXU_COLS), F32)] * n


def _sc_in_body(x_ref, wb_ref, wc_ref, wh_ref, cw_ref, y_ref, wbb_ref, wcb_ref, whb_ref, tail_ref,
                p0_ref, p1_ref, g0_ref, g1_ref):
    _cast_weights_once([(wb_ref, wbb_ref), (wc_ref, wcb_ref), (wh_ref, whb_ref)])
    _reset_tails([tail_ref])
    raw_p, raw_gate = (p0_ref, p1_ref), (g0_ref, g1_ref)

    def issue(unit, slot):
        rows, cs = unit
        n = rows.stop - rows.start
        x = x_ref[rows, :]
        raw_p[slot][0:n, :] = _dot(x, wcb_ref[:, cs]) * _dot(x, whb_ref[:, cs])
        raw_gate[slot][0:n, :] = _dot(x, wbb_ref[:, cs])

    base = _traced_zero()

    def finish(unit, slot):
        rows, cs = unit
        n = rows.stop - rows.start
        for r0 in range(0, n, ROW_CHUNK):
            conv = _conv_rows(raw_p[slot], tail_ref, cs, base, r0, cw_ref)
            gate = _raw_rows(raw_gate[slot], base, r0, ROW_CHUNK)
            y_ref[rows.start + r0:rows.start + r0 + ROW_CHUNK, cs] = (gate * conv).astype(y_ref.dtype)
        tail_ref[:, cs] = _raw_rows(raw_p[slot], base, n - SUBLANES, SUBLANES)

    _pipeline_units(_units(*y_ref.shape), issue, finish)


def _sc_in(xn, w_in, conv_w, layer, tm=2048, tn=512):
    s, d = xn.shape
    c = conv_w.shape[-1]
    nb = c // tn
    part = lambda k: (lambda n: n + k * nb)
    return pl.pallas_call(
        _sc_in_body,
        grid=(nb, s // tm),
        in_specs=[pl.BlockSpec((tm, d), lambda n, m: (m, 0)),
                  _weight_spec(layer, d, tn, part(0)),
                  _weight_spec(layer, d, tn, part(1)),
                  _weight_spec(layer, d, tn, part(2)),
                  _vec_spec(layer, CONV_TAPS, tn, part(0))],
        out_specs=pl.BlockSpec((tm, tn), lambda n, m: (m, n)),
        out_shape=jax.ShapeDtypeStruct((s, c), BF16),
        scratch_shapes=[pltpu.VMEM((d, tn), BF16)] * 3 + [pltpu.VMEM((SUBLANES, tn), F32)] + _raw_slots(4),
        compiler_params=_params(2),
        name="sc_in",
    )(xn, w_in, w_in, w_in, conv_w)


def _ffn_up_body(x_ref, wg_ref, wv_ref, cwg_ref, cwv_ref, bg_ref, bv_ref, a_ref,
                 wgb_ref, wvb_ref, tailg_ref, tailv_ref, g0_ref, g1_ref, v0_ref, v1_ref):
    _cast_weights_once([(wg_ref, wgb_ref), (wv_ref, wvb_ref)])
    _reset_tails([tailg_ref, tailv_ref])
    raw_g, raw_v = (g0_ref, g1_ref), (v0_ref, v1_ref)

    def issue(unit, slot):
        rows, cs = unit
        n = rows.stop - rows.start
        x = x_ref[rows, :]
        raw_g[slot][0:n, :] = _dot(x, wgb_ref[:, cs])
        raw_v[slot][0:n, :] = _dot(x, wvb_ref[:, cs])

    base = _traced_zero()

    def finish(unit, slot):
        rows, cs = unit
        n = rows.stop - rows.start
        for r0 in range(0, n, ROW_CHUNK):
            cg = _conv_rows(raw_g[slot], tailg_ref, cs, base, r0, cwg_ref) + bg_ref[:, cs]
            cv = _conv_rows(raw_v[slot], tailv_ref, cs, base, r0, cwv_ref) + bv_ref[:, cs]
            a_ref[rows.start + r0:rows.start + r0 + ROW_CHUNK, cs] = (
                cg * jax.nn.sigmoid(cg) * cv).astype(a_ref.dtype)
        tailg_ref[:, cs] = _raw_rows(raw_g[slot], base, n - SUBLANES, SUBLANES)
        tailv_ref[:, cs] = _raw_rows(raw_v[slot], base, n - SUBLANES, SUBLANES)

    _pipeline_units(_units(*a_ref.shape), issue, finish)


def _ffn_up(xn, w_up, conv_w, conv_b, layer, tm=2048, tn=512):
    s, d = xn.shape
    f = w_up.shape[-1] // 2
    nb = f // tn
    part = lambda k: (lambda n: n + k * nb)
    conv_b = conv_b.reshape(conv_b.shape[0], 1, 2 * f)
    return pl.pallas_call(
        _ffn_up_body,
        grid=(nb, s // tm),
        in_specs=[pl.BlockSpec((tm, d), lambda n, m: (m, 0)),
                  _weight_spec(layer, d, tn, part(0)), _weight_spec(layer, d, tn, part(1)),
                  _vec_spec(layer, CONV_TAPS, tn, part(0)), _vec_spec(layer, CONV_TAPS, tn, part(1)),
                  _vec_spec(layer, 1, tn, part(0)), _vec_spec(layer, 1, tn, part(1))],
        out_specs=pl.BlockSpec((tm, tn), lambda n, m: (m, n)),
        out_shape=jax.ShapeDtypeStruct((s, f), BF16),
        scratch_shapes=([pltpu.VMEM((d, tn), BF16)] * 2 + [pltpu.VMEM((SUBLANES, tn), F32)] * 2
                        + _raw_slots(4)),
        compiler_params=_params(2),
        name="ffn_up",
    )(xn, w_up, w_up, conv_w, conv_w, conv_b, conv_b)


def _proj_body(x_ref, w_ref, o_ref, wb_ref, r0_ref, r1_ref, *, epilogue, scaled_blocks, scale):
    _cast_weights_once([(w_ref, wb_ref)])
    raw = (r0_ref, r1_ref)
    if epilogue == "qscale":
        factor = jnp.where(pl.program_id(0) < scaled_blocks, scale, 1.0).astype(F32)

    def issue(unit, slot):
        rows, cs = unit
        raw[slot][0:rows.stop - rows.start, :] = _dot(x_ref[rows, :], wb_ref[:, cs])

    base = _traced_zero()

    def finish(unit, slot):
        rows, cs = unit
        for r0 in range(0, rows.stop - rows.start, ROW_CHUNK):
            acc = _raw_rows(raw[slot], base, r0, ROW_CHUNK)
            if epilogue == "gelu":
                acc = 0.5 * acc * (1.0 + lax.erf(acc * (1.0 / math.sqrt(2.0))))
            elif epilogue == "qscale":
                acc = acc * factor
            o_ref[rows.start + r0:rows.start + r0 + ROW_CHUNK, cs] = acc.astype(o_ref.dtype)

    _pipeline_units(_units(*o_ref.shape), issue, finish)


def _proj(xn, w, layer, epilogue, *, scaled_cols=0, scale=1.0, tm=2048, tn=1024, name):
    s, d = xn.shape
    n = w.shape[-1]
    body = functools.partial(_proj_body, epilogue=epilogue, scaled_blocks=scaled_cols // tn, scale=scale)
    return pl.pallas_call(
        body,
        grid=(n // tn, s // tm),
        in_specs=[pl.BlockSpec((tm, d), lambda j, m: (m, 0)),
                  _weight_spec(layer, d, tn, lambda j: j)],
        out_specs=pl.BlockSpec((tm, tn), lambda j, m: (m, j)),
        out_shape=jax.ShapeDtypeStruct((s, n), BF16),
        scratch_shapes=[pltpu.VMEM((d, tn), BF16)] + _raw_slots(2),
        compiler_params=_params(2),
        name=name,
    )(xn, w)


WEIGHT_STAGE_ROWS = 256
OUT_UNIT_ROWS = 128


def _stage_weights(w_hbm, layer, wb_ref, stage_ref, sem):
    n = wb_ref.shape[0] // WEIGHT_STAGE_ROWS

    def copy(c):
        return pltpu.make_async_copy(
            w_hbm.at[layer, pl.ds(c * WEIGHT_STAGE_ROWS, WEIGHT_STAGE_ROWS), :],
            stage_ref.at[c % 2], sem.at[c % 2])

    copy(0).start()
    for c in range(n):
        if c + 1 < n:
            copy(c + 1).start()
        copy(c).wait()
        wb_ref[c * WEIGHT_STAGE_ROWS:(c + 1) * WEIGHT_STAGE_ROWS, :] = stage_ref[c % 2].astype(wb_ref.dtype)


def _out_body(a_ref, w_hbm, h_ref, gp_ref, *rest, layer, emit_next):
    if emit_next:
        gn_ref, ho_ref, xn_ref, wb_ref, stage_ref, sem, raw0_ref, raw1_ref = rest
    else:
        ho_ref, wb_ref, stage_ref, sem, raw0_ref, raw1_ref = rest
    raw = (raw0_ref, raw1_ref)
    tm, d = ho_ref.shape
    unit_rows = raw0_ref.shape[0]

    @pl.when(pl.program_id(0) == 0)
    def _():
        _stage_weights(w_hbm, layer, wb_ref, stage_ref, sem)

    base = jnp.minimum(pl.program_id(0), 0)

    def issue(r, slot):
        a = a_ref[r:r + unit_rows, :]
        for c in range(0, d, 2 * MXU_COLS):
            raw[slot][:, c:c + 2 * MXU_COLS] = _dot(a, wb_ref[:, c:c + 2 * MXU_COLS])

    def finish(r, slot):
        for r0 in range(0, unit_rows, SUBLANES):
            rows = slice(r + r0, r + r0 + SUBLANES)
            h = h_ref[rows, :] + _rms(_raw_rows(raw[slot], base, r0, SUBLANES), gp_ref[...])
            ho_ref[rows, :] = h
            if emit_next:
                xn_ref[rows, :] = _rms(h, gn_ref[...]).astype(xn_ref.dtype)

    _pipeline_units(list(range(0, tm, unit_rows)), issue, finish)


def _out_proj(a, w, layer, h, g_post, g_next, tm, name):
    s, k = a.shape
    d = w.shape[-1]
    emit_next = g_next is not None
    row = lambda width: pl.BlockSpec((tm, width), lambda i: (i, 0))
    vec = pl.BlockSpec((1, d), lambda i: (0, 0))
    in_specs = [row(k), pl.BlockSpec(memory_space=pl.ANY), row(d), vec]
    args = [a, w, h, g_post.reshape(1, d)]
    out_specs, out_shape = [row(d)], [jax.ShapeDtypeStruct((s, d), F32)]
    if emit_next:
        in_specs.append(vec)
        args.append(g_next.reshape(1, d))
        out_specs.append(row(d))
        out_shape.append(jax.ShapeDtypeStruct((s, d), BF16))
    outs = pl.pallas_call(
        functools.partial(_out_body, layer=layer, emit_next=emit_next),
        grid=(s // tm,),
        in_specs=in_specs,
        out_specs=out_specs,
        out_shape=out_shape,
        scratch_shapes=[pltpu.VMEM((k, d), BF16),
                        pltpu.VMEM((2, WEIGHT_STAGE_ROWS, d), F32),
                        pltpu.SemaphoreType.DMA((2,)),
                        pltpu.VMEM((OUT_UNIT_ROWS, d), F32), pltpu.VMEM((OUT_UNIT_ROWS, d), F32)],
        compiler_params=_params(1),
        name=name if emit_next else name + "_last",
    )(*args)
    return (outs[0], outs[1]) if emit_next else (outs[0], None)


def _sgu_body(u_ref, v_ref, g_ref, b_ref, ws_ref, bias_ref, y_ref):
    v = v_ref[...].astype(F32)
    mu = jnp.mean(v, axis=-1, keepdims=True)
    vc = v - mu
    var = jnp.mean(vc * vc, axis=-1, keepdims=True)
    vn = (vc * lax.rsqrt(var + EPS) * g_ref[...] + b_ref[...]).astype(BF16)
    t = SGU_CHUNK
    gw = v.shape[1] // SGU_GROUPS
    tril = (lax.broadcasted_iota(jnp.int32, (t, t), 0) >= lax.broadcasted_iota(jnp.int32, (t, t), 1))
    for g in range(SGU_GROUPS):
        ws = jnp.where(tril, ws_ref[g], 0.0).astype(BF16)
        cols = slice(g * gw, (g + 1) * gw)
        for c in range(v.shape[0] // t):
            rows = slice(c * t, (c + 1) * t)
            mixed = _dot(ws, vn[rows, cols]) + bias_ref[:, cols]
            y_ref[rows, cols] = (u_ref[rows, cols].astype(F32) * mixed).astype(y_ref.dtype)


def _sgu(h, ln_g, ln_b, w_s, bias_full, tm=512):
    s, two_d = h.shape
    d = two_d // 2
    vec = pl.BlockSpec((1, d), lambda m: (0, 0))
    return pl.pallas_call(
        _sgu_body,
        grid=(s // tm,),
        in_specs=[pl.BlockSpec((tm, d), lambda m: (m, 0)),
                  pl.BlockSpec((tm, d), lambda m: (m, 1)),
                  vec, vec,
                  pl.BlockSpec(w_s.shape, lambda m: (0, 0, 0)),
                  pl.BlockSpec(bias_full.shape, lambda m: (0, 0))],
        out_specs=pl.BlockSpec((tm, d), lambda m: (m, 0)),
        out_shape=jax.ShapeDtypeStruct((s, d), BF16),
        compiler_params=_params(1),
        name="sgu_mix",
    )(h, h, ln_g.reshape(1, d), ln_b.reshape(1, d), w_s, bias_full)


class _SbVisit:
    def __init__(self, q_ref, k_ref, v_ref, uo, hd, mask):
        self.cols = slice(hd * SB_HEAD_DIM, (hd + 1) * SB_HEAD_DIM)
        self.q_ref, self.k_ref, self.v_ref, self.uo, self.hd, self.mask = q_ref, k_ref, v_ref, uo, hd, mask

    def scores(self):
        q = self.q_ref[:, self.cols]
        kb = self.k_ref[:, self.cols]
        self.z = lax.dot_general(q, kb, (((1,), (1,)), ((), ())), preferred_element_type=F32)

    def log_terms(self):
        z = self.z
        self.zl = jnp.minimum(z, 0.0) - jnp.log(1.0 + jnp.exp(-jnp.abs(z)))
        l = self.zl - z
        if self.mask is not None:
            l = jnp.where(self.mask, l, 0.0)
        self.l = l.astype(BF16)

    def tile_sums(self):
        self.sums = _dot(self.l, self.uo)

    def weights(self, carry):
        t = SB_TILE
        a = jnp.exp(self.zl + self.sums[:, :t] + jnp.tile(carry, (1, t // LANES)))
        if self.mask is not None:
            a = jnp.where(self.mask, a, 0.0)
        self.a = a.astype(BF16)
        return carry + self.sums[:, t:]

    def values(self, acc):
        return acc + _dot(self.a, self.v_ref[:, self.cols])

    STAGES = 5

    def run_stage(self, s, acc, carry):
        if s == 0:
            self.scores()
        elif s == 1:
            self.log_terms()
        elif s == 2:
            self.tile_sums()
        elif s == 3:
            carry[self.hd] = self.weights(carry[self.hd])
        else:
            acc[self.hd] = self.values(acc[self.hd])


def _sb_pipeline(visits, acc, carry):
    for step in range(len(visits) + _SbVisit.STAGES - 1):
        for s in reversed(range(_SbVisit.STAGES)):
            v = step - s
            if 0 <= v < len(visits):
                visits[v].run_stage(s, acc, carry)


def _sb_attn_body(q_ref, kd_ref, kp_ref, vd_ref, vp_ref, qkv_hbm, o_ref,
                  acc_ref, carry_ref, kbuf_ref, vbuf_ref, sem, *, n_groups):
    t = SB_TILE
    g = pl.program_id(0)
    i = pl.program_id(1)
    r = lax.broadcasted_iota(jnp.int32, (t, t), 0)
    c = lax.broadcasted_iota(jnp.int32, (t, t), 1)
    causal = c < r
    ur = lax.broadcasted_iota(jnp.int32, (t, t + LANES), 0)
    uc = lax.broadcasted_iota(jnp.int32, (t, t + LANES), 1)
    uo = jnp.where((ur > uc) | (uc >= t), 1.0, 0.0).astype(BF16)
    heads = range(SB_HEAD_GROUP)
    zeros = jnp.zeros((t, SB_HEAD_DIM), F32)

    def visit_tiles(tiles, acc, carry):
        visits = [_SbVisit(q_ref, k, v, uo, hd, mask) for k, v, mask in tiles for hd in heads]
        _sb_pipeline(visits, acc, carry)
        for hd in heads:
            acc_ref[hd], carry_ref[hd] = acc[hd], carry[hd]

    def fresh():
        return {hd: zeros for hd in heads}, {hd: zeros for hd in heads}

    @pl.when(i > 0)
    def _():
        visit_tiles([(kd_ref, vd_ref, causal), (kp_ref, vp_ref, None)], *fresh())

    @pl.when(i == 0)
    def _():
        visit_tiles([(kd_ref, vd_ref, causal)], *fresh())

    def fetch(j, part, buf_ref, slot):
        gw = buf_ref.shape[1]
        return pltpu.make_async_copy(
            qkv_hbm.at[pl.ds(pl.multiple_of(j * t, t), t),
                       pl.ds(pl.multiple_of((part * n_groups + g) * gw, LANES), gw)],
            buf_ref, sem.at[slot])

    def cond(state):
        j, cmax = state
        return jnp.logical_and(j >= 0, cmax > -SB_EXP_ZERO)

    def body(state):
        j, _ = state
        copies = [fetch(j, 1, kbuf_ref, 0), fetch(j, 2, vbuf_ref, 1)]
        for c in copies:
            c.start()
        for c in copies:
            c.wait()
        visit_tiles([(kbuf_ref, vbuf_ref, None)],
                    {hd: acc_ref[hd] for hd in heads}, {hd: carry_ref[hd] for hd in heads})
        return j - 1, jnp.max(carry_ref[...])

    lax.while_loop(cond, body, (i - 2, jnp.max(carry_ref[...])))
    for hd in heads:
        o_ref[:, hd * SB_HEAD_DIM:(hd + 1) * SB_HEAD_DIM] = acc_ref[hd].astype(o_ref.dtype)


def _sb_attn(qkv):
    s = qkv.shape[0]
    t = SB_TILE
    gw = SB_HEAD_GROUP * SB_HEAD_DIM
    ng = SB_HEADS // SB_HEAD_GROUP
    diag = lambda part: pl.BlockSpec((t, gw), lambda g, i: (i, part * ng + g))
    prev = lambda part: pl.BlockSpec((t, gw), lambda g, i: (jnp.maximum(i - 1, 0), part * ng + g))
    return pl.pallas_call(
        functools.partial(_sb_attn_body, n_groups=ng),
        grid=(ng, s // t),
        in_specs=[diag(0), diag(1), prev(1), diag(2), prev(2), pl.BlockSpec(memory_space=pl.ANY)],
        out_specs=pl.BlockSpec((t, gw), lambda g, i: (i, g)),
        out_shape=jax.ShapeDtypeStruct((s, SB_HEADS * SB_HEAD_DIM), BF16),
        scratch_shapes=[pltpu.VMEM((SB_HEAD_GROUP, t, SB_HEAD_DIM), F32),
                        pltpu.VMEM((SB_HEAD_GROUP, t, LANES), F32),
                        pltpu.VMEM((t, gw), BF16), pltpu.VMEM((t, gw), BF16),
                        pltpu.SemaphoreType.DMA((2,))],
        compiler_params=_params(2),
        name="sb_attn",
    )(qkv, qkv, qkv, qkv, qkv, qkv)


def kernel(x, norm_mix_pre, norm_mix_post, norm_ffn_pre, norm_ffn_post, sc_w_in, sc_conv_w, sc_w_out, sg_w_in, sg_ln_g, sg_ln_b, sg_w_s, sg_b_s, sg_w_out, sb_w_qkv, sb_w_out, ffn_w_up, ffn_conv_w, ffn_conv_b, ffn_w_down):
    batch, seq, d = x.shape
    assert batch == 1, "row tiles carry the causal conv state across the flattened sequence"
    depth = norm_mix_pre.shape[0]
    n_mixers = 3

    h = x.reshape(seq, d)
    xn = _rmsnorm(h, norm_mix_pre[0])
    for i in range(depth):
        kind, j = i % n_mixers, i // n_mixers
        if kind == 0:
            a = _sc_in(xn, sc_w_in, sc_conv_w, j)
            w_out = sc_w_out
        elif kind == 1:
            hid = _proj(xn, sg_w_in, j, "gelu", name="sg_in")
            bias_full = jnp.repeat(sg_b_s[j].T, hid.shape[1] // 2 // SGU_GROUPS, axis=1)
            a = _sgu(hid, sg_ln_g[j], sg_ln_b[j], sg_w_s[j], bias_full)
            w_out = sg_w_out
        else:
            qkv = _proj(xn, sb_w_qkv, j, "qscale", scaled_cols=SB_HEADS * SB_HEAD_DIM,
                        scale=SB_HEAD_DIM ** -0.5, name="sb_qkv")
            a = _sb_attn(qkv)
            w_out = sb_w_out
        h, xn = _out_proj(a, w_out, j, h, norm_mix_post[i], norm_ffn_pre[i], 512, "mix_out")
        act = _ffn_up(xn, ffn_w_up, ffn_conv_w, ffn_conv_b, i)
        g_next = norm_mix_pre[i + 1] if i + 1 < depth else None
        h, xn = _out_proj(act, ffn_w_down, i, h, norm_ffn_post[i], g_next, 256, "ffn_down")
    return h.reshape(batch, seq, d)
```

```python
import functools
import math

import jax
import jax.numpy as jnp
from jax import lax
from jax.experimental import pallas as pl
from jax.experimental.pallas import tpu as pltpu

EPS = 1e-6
CONV_TAPS = 3
SUBLANES = 8
LANES = 128
MXU_COLS = 256
PAIR_COLS = 2 * MXU_COLS
UNIT_ROWS = 512
ROW_CHUNK = 64
SGU_CHUNK = 128
SGU_GROUPS = 8
SB_HEADS = 16
SB_HEAD_DIM = 128
SB_TILE = 256
SB_HEAD_GROUP = 16
SB_EXP_ZERO = 106.0
MIB = 1024 * 1024
VMEM_LIMIT = 56 * MIB

F32 = jnp.float32
BF16 = jnp.bfloat16


def _params(n_axes, vmem=VMEM_LIMIT):
    return pltpu.CompilerParams(
        dimension_semantics=("arbitrary",) * n_axes, vmem_limit_bytes=vmem)


def _dot(a, b):
    return jnp.dot(a, b, preferred_element_type=F32)


def _rms(x, g):
    ms = jnp.mean(x * x, axis=-1, keepdims=True)
    return x * lax.rsqrt(ms + EPS) * g


def _rmsnorm_body(x_ref, g_ref, o_ref):
    o_ref[...] = _rms(x_ref[...], g_ref[...]).astype(o_ref.dtype)


def _rmsnorm(x, g, tm=1024):
    s, d = x.shape
    return pl.pallas_call(
        _rmsnorm_body,
        grid=(s // tm,),
        in_specs=[pl.BlockSpec((tm, d), lambda m: (m, 0)),
                  pl.BlockSpec((1, d), lambda m: (0, 0))],
        out_specs=pl.BlockSpec((tm, d), lambda m: (m, 0)),
        out_shape=jax.ShapeDtypeStruct((s, d), BF16),
        compiler_params=_params(1),
        name="rmsnorm_first",
    )(x, g.reshape(1, d))


def _weight_spec(layer, d, tn, col_block):
    return pl.BlockSpec((None, d, tn), lambda n, m: (layer, 0, col_block(n)))


def _vec_spec(layer, rows, tn, col_block):
    return pl.BlockSpec((None, rows, tn), lambda n, m: (layer, 0, col_block(n)))


def _cast_weights_once(pairs):
    @pl.when(pl.program_id(1) == 0)
    def _():
        for src, dst in pairs:
            dst[...] = src[...].astype(dst.dtype)


def _units(tm, tn, cols=MXU_COLS):
    return [(slice(r, r + UNIT_ROWS), slice(c, c + cols))
            for c in range(0, tn, cols) for r in range(0, tm, UNIT_ROWS)]


def _pipeline_units(units, issue, finish):
    issue(units[0], 0)
    for k, unit in enumerate(units):
        if k + 1 < len(units):
            issue(units[k + 1], (k + 1) % 2)
        finish(unit, k % 2)


def _traced_zero():
    return jnp.minimum(pl.program_id(1), 0)


def _raw_rows(raw_ref, base, r0, n):
    return raw_ref[pl.ds(pl.multiple_of(base + r0, SUBLANES), n), :]


def _conv_rows(raw_ref, tail_ref, cs, base, r0, w_ref):
    cur = _raw_rows(raw_ref, base, r0, ROW_CHUNK)
    prev = tail_ref[:, cs] if r0 == 0 else _raw_rows(raw_ref, base, r0 - SUBLANES, SUBLANES)
    ext = jnp.concatenate([prev, cur], axis=0)
    s1 = pltpu.roll(ext, 1, 0)[SUBLANES:]
    s2 = pltpu.roll(ext, 2, 0)[SUBLANES:]
    return w_ref[2:3, cs] * cur + w_ref[1:2, cs] * s1 + w_ref[0:1, cs] * s2


def _reset_tails(tail_refs):
    @pl.when(pl.program_id(1) == 0)
    def _():
        for t in tail_refs:
            t[...] = jnp.zeros(t.shape, t.dtype)


def _raw_slots(n, cols=MXU_COLS):
    return [pltpu.VMEM((UNIT_ROWS, cols), F32)] * n


def _sc_in_body(x_ref, wb_ref, wc_ref, wh_ref, cw_ref, y_ref, wbb_ref, wcb_ref, whb_ref, tail_ref,
                p0_ref, p1_ref, g0_ref, g1_ref):
    _cast_weights_once([(wb_ref, wbb_ref), (wc_ref, wcb_ref), (wh_ref, whb_ref)])
    _reset_tails([tail_ref])
    raw_p, raw_gate = (p0_ref, p1_ref), (g0_ref, g1_ref)

    def issue(unit, slot):
        rows, cs = unit
        x = x_ref[rows, :]
        raw_p[slot][...] = _dot(x, wcb_ref[:, cs]) * _dot(x, whb_ref[:, cs])
        raw_gate[slot][...] = _dot(x, wbb_ref[:, cs])

    base = _traced_zero()

    def finish(unit, slot):
        rows, cs = unit
        for r0 in range(0, UNIT_ROWS, ROW_CHUNK):
            conv = _conv_rows(raw_p[slot], tail_ref, cs, base, r0, cw_ref)
            gate = _raw_rows(raw_gate[slot], base, r0, ROW_CHUNK)
            y_ref[rows.start + r0:rows.start + r0 + ROW_CHUNK, cs] = (gate * conv).astype(y_ref.dtype)
        tail_ref[:, cs] = _raw_rows(raw_p[slot], base, UNIT_ROWS - SUBLANES, SUBLANES)

    _pipeline_units(_units(*y_ref.shape), issue, finish)


def _sc_in(xn, w_in, conv_w, layer, tm=2048, tn=512):
    s, d = xn.shape
    c = conv_w.shape[-1]
    nb = c // tn
    part = lambda k: (lambda n: n + k * nb)
    return pl.pallas_call(
        _sc_in_body,
        grid=(nb, s // tm),
        in_specs=[pl.BlockSpec((tm, d), lambda n, m: (m, 0)),
                  _weight_spec(layer, d, tn, part(0)),
                  _weight_spec(layer, d, tn, part(1)),
                  _weight_spec(layer, d, tn, part(2)),
                  _vec_spec(layer, CONV_TAPS, tn, part(0))],
        out_specs=pl.BlockSpec((tm, tn), lambda n, m: (m, n)),
        out_shape=jax.ShapeDtypeStruct((s, c), BF16),
        scratch_shapes=[pltpu.VMEM((d, tn), BF16)] * 3 + [pltpu.VMEM((SUBLANES, tn), F32)] + _raw_slots(4),
        compiler_params=_params(2),
        name="sc_in",
    )(xn, w_in, w_in, w_in, conv_w)


def _ffn_up_body(x_ref, wg_ref, wv_ref, cwg_ref, cwv_ref, bg_ref, bv_ref, a_ref,
                 wgb_ref, wvb_ref, tailg_ref, tailv_ref, g0_ref, g1_ref, v0_ref, v1_ref):
    _cast_weights_once([(wg_ref, wgb_ref), (wv_ref, wvb_ref)])
    _reset_tails([tailg_ref, tailv_ref])
    raw_g, raw_v = (g0_ref, g1_ref), (v0_ref, v1_ref)

    def issue(unit, slot):
        rows, cs = unit
        x = x_ref[rows, :]
        raw_g[slot][...] = _dot(x, wgb_ref[:, cs])
        raw_v[slot][...] = _dot(x, wvb_ref[:, cs])

    base = _traced_zero()

    def finish(unit, slot):
        rows, cs = unit
        for r0 in range(0, UNIT_ROWS, ROW_CHUNK):
            cg = _conv_rows(raw_g[slot], tailg_ref, cs, base, r0, cwg_ref) + bg_ref[:, cs]
            cv = _conv_rows(raw_v[slot], tailv_ref, cs, base, r0, cwv_ref) + bv_ref[:, cs]
            a_ref[rows.start + r0:rows.start + r0 + ROW_CHUNK, cs] = (
                cg * jax.nn.sigmoid(cg) * cv).astype(a_ref.dtype)
        tailg_ref[:, cs] = _raw_rows(raw_g[slot], base, UNIT_ROWS - SUBLANES, SUBLANES)
        tailv_ref[:, cs] = _raw_rows(raw_v[slot], base, UNIT_ROWS - SUBLANES, SUBLANES)

    _pipeline_units(_units(*a_ref.shape), issue, finish)


def _ffn_up(xn, w_up, conv_w, conv_b, layer, tm=2048, tn=512):
    s, d = xn.shape
    f = w_up.shape[-1] // 2
    nb = f // tn
    part = lambda k: (lambda n: n + k * nb)
    conv_b = conv_b.reshape(conv_b.shape[0], 1, 2 * f)
    return pl.pallas_call(
        _ffn_up_body,
        grid=(nb, s // tm),
        in_specs=[pl.BlockSpec((tm, d), lambda n, m: (m, 0)),
                  _weight_spec(layer, d, tn, part(0)), _weight_spec(layer, d, tn, part(1)),
                  _vec_spec(layer, CONV_TAPS, tn, part(0)), _vec_spec(layer, CONV_TAPS, tn, part(1)),
                  _vec_spec(layer, 1, tn, part(0)), _vec_spec(layer, 1, tn, part(1))],
        out_specs=pl.BlockSpec((tm, tn), lambda n, m: (m, n)),
        out_shape=jax.ShapeDtypeStruct((s, f), BF16),
        scratch_shapes=([pltpu.VMEM((d, tn), BF16)] * 2 + [pltpu.VMEM((SUBLANES, tn), F32)] * 2
                        + _raw_slots(4)),
        compiler_params=_params(2),
        name="ffn_up",
    )(xn, w_up, w_up, conv_w, conv_w, conv_b, conv_b)


def _proj_body(x_ref, w_ref, o_ref, wb_ref, r0_ref, r1_ref, *, epilogue, scaled_blocks, scale):
    _cast_weights_once([(w_ref, wb_ref)])
    raw = (r0_ref, r1_ref)
    if epilogue == "qscale":
        factor = jnp.where(pl.program_id(0) < scaled_blocks, scale, 1.0).astype(F32)

    def issue(unit, slot):
        rows, cs = unit
        raw[slot][...] = _dot(x_ref[rows, :], wb_ref[:, cs])

    base = _traced_zero()

    def finish(unit, slot):
        rows, cs = unit
        for r0 in range(0, UNIT_ROWS, ROW_CHUNK):
            acc = _raw_rows(raw[slot], base, r0, ROW_CHUNK)
            if epilogue == "gelu":
                acc = 0.5 * acc * (1.0 + lax.erf(acc * (1.0 / math.sqrt(2.0))))
            elif epilogue == "qscale":
                acc = acc * factor
            o_ref[rows.start + r0:rows.start + r0 + ROW_CHUNK, cs] = acc.astype(o_ref.dtype)

    _pipeline_units(_units(*o_ref.shape, cols=PAIR_COLS), issue, finish)


def _proj(xn, w, layer, epilogue, *, scaled_cols=0, scale=1.0, tm=2048, tn=1024, name):
    s, d = xn.shape
    n = w.shape[-1]
    body = functools.partial(_proj_body, epilogue=epilogue, scaled_blocks=scaled_cols // tn, scale=scale)
    return pl.pallas_call(
        body,
        grid=(n // tn, s // tm),
        in_specs=[pl.BlockSpec((tm, d), lambda j, m: (m, 0)),
                  _weight_spec(layer, d, tn, lambda j: j)],
        out_specs=pl.BlockSpec((tm, tn), lambda j, m: (m, j)),
        out_shape=jax.ShapeDtypeStruct((s, n), BF16),
        scratch_shapes=[pltpu.VMEM((d, tn), BF16)] + _raw_slots(2, cols=PAIR_COLS),
        compiler_params=_params(2),
        name=name,
    )(xn, w)


WEIGHT_STAGE_ROWS = 512


def _stage_weights(w_hbm, layer, wb_ref, stage_ref, sem):
    n = wb_ref.shape[0] // WEIGHT_STAGE_ROWS

    def copy(c):
        return pltpu.make_async_copy(
            w_hbm.at[layer, pl.ds(c * WEIGHT_STAGE_ROWS, WEIGHT_STAGE_ROWS), :],
            stage_ref.at[c % 2], sem.at[c % 2])

    copy(0).start()
    for c in range(n):
        if c + 1 < n:
            copy(c + 1).start()
        copy(c).wait()
        wb_ref[c * WEIGHT_STAGE_ROWS:(c + 1) * WEIGHT_STAGE_ROWS, :] = stage_ref[c % 2].astype(wb_ref.dtype)


def _out_body(a_ref, w_hbm, h_ref, gp_ref, *rest, layer, emit_next):
    if emit_next:
        gn_ref, ho_ref, xn_ref, wb_ref, stage_ref, sem = rest
    else:
        ho_ref, wb_ref, stage_ref, sem = rest

    @pl.when(pl.program_id(0) == 0)
    def _():
        _stage_weights(w_hbm, layer, wb_ref, stage_ref, sem)

    h = h_ref[...] + _rms(_dot(a_ref[...], wb_ref[...]), gp_ref[...])
    ho_ref[...] = h
    if emit_next:
        xn_ref[...] = _rms(h, gn_ref[...]).astype(xn_ref.dtype)


def _out_proj(a, w, layer, h, g_post, g_next, tm, name):
    s, k = a.shape
    d = w.shape[-1]
    emit_next = g_next is not None
    row = lambda width: pl.BlockSpec((tm, width), lambda i: (i, 0))
    vec = pl.BlockSpec((1, d), lambda i: (0, 0))
    in_specs = [row(k), pl.BlockSpec(memory_space=pl.ANY), row(d), vec]
    args = [a, w, h, g_post.reshape(1, d)]
    out_specs, out_shape = [row(d)], [jax.ShapeDtypeStruct((s, d), F32)]
    if emit_next:
        in_specs.append(vec)
        args.append(g_next.reshape(1, d))
        out_specs.append(row(d))
        out_shape.append(jax.ShapeDtypeStruct((s, d), BF16))
    outs = pl.pallas_call(
        functools.partial(_out_body, layer=layer, emit_next=emit_next),
        grid=(s // tm,),
        in_specs=in_specs,
        out_specs=out_specs,
        out_shape=out_shape,
        scratch_shapes=[pltpu.VMEM((k, d), BF16),
                        pltpu.VMEM((2, WEIGHT_STAGE_ROWS, d), F32),
                        pltpu.SemaphoreType.DMA((2,))],
        compiler_params=_params(1),
        name=name if emit_next else name + "_last",
    )(*args)
    return (outs[0], outs[1]) if emit_next else (outs[0], None)


def _sgu_body(u_ref, v_ref, g_ref, b_ref, ws_ref, bias_ref, y_ref):
    v = v_ref[...].astype(F32)
    mu = jnp.mean(v, axis=-1, keepdims=True)
    vc = v - mu
    var = jnp.mean(vc * vc, axis=-1, keepdims=True)
    vn = (vc * lax.rsqrt(var + EPS) * g_ref[...] + b_ref[...]).astype(BF16)
    t = SGU_CHUNK
    gw = v.shape[1] // SGU_GROUPS
    tril = (lax.broadcasted_iota(jnp.int32, (t, t), 0) >= lax.broadcasted_iota(jnp.int32, (t, t), 1))
    for g in range(SGU_GROUPS):
        ws = jnp.where(tril, ws_ref[g], 0.0).astype(BF16)
        cols = slice(g * gw, (g + 1) * gw)
        for c in range(v.shape[0] // t):
            rows = slice(c * t, (c + 1) * t)
            mixed = _dot(ws, vn[rows, cols]) + bias_ref[:, cols]
            y_ref[rows, cols] = (u_ref[rows, cols].astype(F32) * mixed).astype(y_ref.dtype)


def _sgu(h, ln_g, ln_b, w_s, bias_full, tm=1024):
    s, two_d = h.shape
    d = two_d // 2
    vec = pl.BlockSpec((1, d), lambda m: (0, 0))
    return pl.pallas_call(
        _sgu_body,
        grid=(s // tm,),
        in_specs=[pl.BlockSpec((tm, d), lambda m: (m, 0)),
                  pl.BlockSpec((tm, d), lambda m: (m, 1)),
                  vec, vec,
                  pl.BlockSpec(w_s.shape, lambda m: (0, 0, 0)),
                  pl.BlockSpec(bias_full.shape, lambda m: (0, 0))],
        out_specs=pl.BlockSpec((tm, d), lambda m: (m, 0)),
        out_shape=jax.ShapeDtypeStruct((s, d), BF16),
        compiler_params=_params(1),
        name="sgu_mix",
    )(h, h, ln_g.reshape(1, d), ln_b.reshape(1, d), w_s, bias_full)


class _SbVisit:
    def __init__(self, q_ref, k_ref, v_ref, uo, hd, mask):
        self.cols = slice(hd * SB_HEAD_DIM, (hd + 1) * SB_HEAD_DIM)
        self.q_ref, self.k_ref, self.v_ref, self.uo, self.hd, self.mask = q_ref, k_ref, v_ref, uo, hd, mask

    def scores(self):
        q = self.q_ref[:, self.cols]
        kb = self.k_ref[:, self.cols]
        self.z = lax.dot_general(q, kb, (((1,), (1,)), ((), ())), preferred_element_type=F32)

    def log_terms(self):
        z = self.z
        self.zl = jnp.minimum(z, 0.0) - jnp.log(1.0 + jnp.exp(-jnp.abs(z)))
        l = self.zl - z
        if self.mask is not None:
            l = jnp.where(self.mask, l, 0.0)
        self.l = l.astype(BF16)

    def tile_sums(self):
        self.sums = _dot(self.l, self.uo)

    def weights(self, carry):
        t = SB_TILE
        a = jnp.exp(self.zl + self.sums[:, :t] + jnp.tile(carry, (1, t // LANES)))
        if self.mask is not None:
            a = jnp.where(self.mask, a, 0.0)
        self.a = a.astype(BF16)
        return carry + self.sums[:, t:]

    def values(self, acc):
        return acc + _dot(self.a, self.v_ref[:, self.cols])

    STAGES = 5

    def run_stage(self, s, acc, carry):
        if s == 0:
            self.scores()
        elif s == 1:
            self.log_terms()
        elif s == 2:
            self.tile_sums()
        elif s == 3:
            carry[self.hd] = self.weights(carry[self.hd])
        else:
            acc[self.hd] = self.values(acc[self.hd])


def _sb_pipeline(visits, acc, carry):
    for step in range(len(visits) + _SbVisit.STAGES - 1):
        for s in reversed(range(_SbVisit.STAGES)):
            v = step - s
            if 0 <= v < len(visits):
                visits[v].run_stage(s, acc, carry)


def _sb_attn_body(q_ref, kd_ref, kp_ref, vd_ref, vp_ref, qkv_hbm, o_ref,
                  acc_ref, carry_ref, kbuf_ref, vbuf_ref, sem, *, n_groups):
    t = SB_TILE
    g = pl.program_id(0)
    i = pl.program_id(1)
    r = lax.broadcasted_iota(jnp.int32, (t, t), 0)
    c = lax.broadcasted_iota(jnp.int32, (t, t), 1)
    causal = c < r
    ur = lax.broadcasted_iota(jnp.int32, (t, t + LANES), 0)
    uc = lax.broadcasted_iota(jnp.int32, (t, t + LANES), 1)
    uo = jnp.where((ur > uc) | (uc >= t), 1.0, 0.0).astype(BF16)
    heads = range(SB_HEAD_GROUP)
    zeros = jnp.zeros((t, SB_HEAD_DIM), F32)

    def visit_tiles(tiles, acc, carry):
        visits = [_SbVisit(q_ref, k, v, uo, hd, mask) for k, v, mask in tiles for hd in heads]
        _sb_pipeline(visits, acc, carry)
        for hd in heads:
            acc_ref[hd], carry_ref[hd] = acc[hd], carry[hd]

    def fresh():
        return {hd: zeros for hd in heads}, {hd: zeros for hd in heads}

    @pl.when(i > 0)
    def _():
        visit_tiles([(kd_ref, vd_ref, causal), (kp_ref, vp_ref, None)], *fresh())

    @pl.when(i == 0)
    def _():
        visit_tiles([(kd_ref, vd_ref, causal)], *fresh())

    def fetch(j, part, buf_ref, slot):
        gw = buf_ref.shape[1]
        return pltpu.make_async_copy(
            qkv_hbm.at[pl.ds(pl.multiple_of(j * t, t), t),
                       pl.ds(pl.multiple_of((part * n_groups + g) * gw, LANES), gw)],
            buf_ref, sem.at[slot])

    def cond(state):
        j, cmax = state
        return jnp.logical_and(j >= 0, cmax > -SB_EXP_ZERO)

    def body(state):
        j, _ = state
        copies = [fetch(j, 1, kbuf_ref, 0), fetch(j, 2, vbuf_ref, 1)]
        for c in copies:
            c.start()
        for c in copies:
            c.wait()
        visit_tiles([(kbuf_ref, vbuf_ref, None)],
                    {hd: acc_ref[hd] for hd in heads}, {hd: carry_ref[hd] for hd in heads})
        return j - 1, jnp.max(carry_ref[...])

    lax.while_loop(cond, body, (i - 2, jnp.max(carry_ref[...])))
    for hd in heads:
        o_ref[:, hd * SB_HEAD_DIM:(hd + 1) * SB_HEAD_DIM] = acc_ref[hd].astype(o_ref.dtype)


def _sb_attn(qkv):
    s = qkv.shape[0]
    t = SB_TILE
    gw = SB_HEAD_GROUP * SB_HEAD_DIM
    ng = SB_HEADS // SB_HEAD_GROUP
    diag = lambda part: pl.BlockSpec((t, gw), lambda g, i: (i, part * ng + g))
    prev = lambda part: pl.BlockSpec((t, gw), lambda g, i: (jnp.maximum(i - 1, 0), part * ng + g))
    return pl.pallas_call(
        functools.partial(_sb_attn_body, n_groups=ng),
        grid=(ng, s // t),
        in_specs=[diag(0), diag(1), prev(1), diag(2), prev(2), pl.BlockSpec(memory_space=pl.ANY)],
        out_specs=pl.BlockSpec((t, gw), lambda g, i: (i, g)),
        out_shape=jax.ShapeDtypeStruct((s, SB_HEADS * SB_HEAD_DIM), BF16),
        scratch_shapes=[pltpu.VMEM((SB_HEAD_GROUP, t, SB_HEAD_DIM), F32),
                        pltpu.VMEM((SB_HEAD_GROUP, t, LANES), F32),
                        pltpu.VMEM((t, gw), BF16), pltpu.VMEM((t, gw), BF16),
                        pltpu.SemaphoreType.DMA((2,))],
        compiler_params=_params(2),
        name="sb_attn",
    )(qkv, qkv, qkv, qkv, qkv, qkv)


def kernel(x, norm_mix_pre, norm_mix_post, norm_ffn_pre, norm_ffn_post, sc_w_in, sc_conv_w, sc_w_out, sg_w_in, sg_ln_g, sg_ln_b, sg_w_s, sg_b_s, sg_w_out, sb_w_qkv, sb_w_out, ffn_w_up, ffn_conv_w, ffn_conv_b, ffn_w_down):
    batch, seq, d = x.shape
    assert batch == 1, "row tiles carry the causal conv state across the flattened sequence"
    depth = norm_mix_pre.shape[0]
    n_mixers = 3

    h = x.reshape(seq, d)
    xn = _rmsnorm(h, norm_mix_pre[0])
    for i in range(depth):
        kind, j = i % n_mixers, i // n_mixers
        if kind == 0:
            a = _sc_in(xn, sc_w_in, sc_conv_w, j)
            w_out = sc_w_out
        elif kind == 1:
            hid = _proj(xn, sg_w_in, j, "gelu", name="sg_in")
            bias_full = jnp.repeat(sg_b_s[j].T, hid.shape[1] // 2 // SGU_GROUPS, axis=1)
            a = _sgu(hid, sg_ln_g[j], sg_ln_b[j], sg_w_s[j], bias_full)
            w_out = sg_w_out
        else:
            qkv = _proj(xn, sb_w_qkv, j, "qscale", scaled_cols=SB_HEADS * SB_HEAD_DIM,
                        scale=SB_HEAD_DIM ** -0.5, name="sb_qkv")
            a = _sb_attn(qkv)
            w_out = sb_w_out
        h, xn = _out_proj(a, w_out, j, h, norm_mix_post[i], norm_ffn_pre[i], 512, "mix_out")
        act = _ffn_up(xn, ffn_w_up, ffn_conv_w, ffn_conv_b, i)
        g_next = norm_mix_pre[i + 1] if i + 1 < depth else None
        h, xn = _out_proj(act, ffn_w_down, i, h, norm_ffn_post[i], g_next, 256, "ffn_down")
    return h.reshape(batch, seq, d)
```

```python
import functools
import math

import jax
import jax.numpy as jnp
from jax import lax
from jax.experimental import pallas as pl
from jax.experimental.pallas import tpu as pltpu

EPS = 1e-6
CONV_TAPS = 3
SUBLANES = 8
LANES = 128
MXU_COLS = 256
PAIR_COLS = 2 * MXU_COLS
UNIT_ROWS = 512
ROW_CHUNK = 64
SGU_CHUNK = 128
SGU_GROUPS = 8
SB_HEADS = 16
SB_HEAD_DIM = 128
SB_TILE = 256
SB_HEAD_GROUP = 16
SB_EXP_ZERO = 106.0
MIB = 1024 * 1024
VMEM_LIMIT = 56 * MIB

F32 = jnp.float32
BF16 = jnp.bfloat16


def _params(n_axes, vmem=VMEM_LIMIT):
    return pltpu.CompilerParams(
        dimension_semantics=("arbitrary",) * n_axes, vmem_limit_bytes=vmem)


def _dot(a, b):
    return jnp.dot(a, b, preferred_element_type=F32)


def _rms(x, g):
    ms = jnp.mean(x * x, axis=-1, keepdims=True)
    return x * lax.rsqrt(ms + EPS) * g


def _rmsnorm_body(x_ref, g_ref, o_ref):
    o_ref[...] = _rms(x_ref[...], g_ref[...]).astype(o_ref.dtype)


def _rmsnorm(x, g, tm=1024):
    s, d = x.shape
    return pl.pallas_call(
        _rmsnorm_body,
        grid=(s // tm,),
        in_specs=[pl.BlockSpec((tm, d), lambda m: (m, 0)),
                  pl.BlockSpec((1, d), lambda m: (0, 0))],
        out_specs=pl.BlockSpec((tm, d), lambda m: (m, 0)),
        out_shape=jax.ShapeDtypeStruct((s, d), BF16),
        compiler_params=_params(1),
        name="rmsnorm_first",
    )(x, g.reshape(1, d))


def _weight_spec(layer, d, tn, col_block):
    return pl.BlockSpec((None, d, tn), lambda n, m: (layer, 0, col_block(n)))


def _vec_spec(layer, rows, tn, col_block):
    return pl.BlockSpec((None, rows, tn), lambda n, m: (layer, 0, col_block(n)))


def _cast_weights_once(pairs):
    @pl.when(pl.program_id(1) == 0)
    def _():
        for src, dst in pairs:
            dst[...] = src[...].astype(dst.dtype)


def _units(tm, tn, cols=MXU_COLS):
    return [(slice(r, r + UNIT_ROWS), slice(c, c + cols))
            for c in range(0, tn, cols) for r in range(0, tm, UNIT_ROWS)]


def _pipeline_units(units, issue, finish):
    issue(units[0], 0)
    for k, unit in enumerate(units):
        if k + 1 < len(units):
            issue(units[k + 1], (k + 1) % 2)
        finish(unit, k % 2)


def _traced_zero():
    return jnp.minimum(pl.program_id(1), 0)


def _raw_rows(raw_ref, base, r0, n):
    return raw_ref[pl.ds(pl.multiple_of(base + r0, SUBLANES), n), :]


def _conv_rows(raw_ref, tail_ref, cs, base, r0, w_ref):
    cur = _raw_rows(raw_ref, base, r0, ROW_CHUNK)
    prev = tail_ref[:, cs] if r0 == 0 else _raw_rows(raw_ref, base, r0 - SUBLANES, SUBLANES)
    ext = jnp.concatenate([prev, cur], axis=0)
    s1 = pltpu.roll(ext, 1, 0)[SUBLANES:]
    s2 = pltpu.roll(ext, 2, 0)[SUBLANES:]
    return w_ref[2:3, cs] * cur + w_ref[1:2, cs] * s1 + w_ref[0:1, cs] * s2


def _reset_tails(tail_refs):
    @pl.when(pl.program_id(1) == 0)
    def _():
        for t in tail_refs:
            t[...] = jnp.zeros(t.shape, t.dtype)


def _raw_slots(n, cols=MXU_COLS):
    return [pltpu.VMEM((UNIT_ROWS, cols), F32)] * n


def _sc_in_body(x_ref, wb_ref, wc_ref, wh_ref, cw_ref, y_ref, wbb_ref, wcb_ref, whb_ref, tail_ref,
                p0_ref, p1_ref, g0_ref, g1_ref):
    _cast_weights_once([(wb_ref, wbb_ref), (wc_ref, wcb_ref), (wh_ref, whb_ref)])
    _reset_tails([tail_ref])
    raw_p, raw_gate = (p0_ref, p1_ref), (g0_ref, g1_ref)

    def issue(unit, slot):
        rows, cs = unit
        x = x_ref[rows, :]
        raw_p[slot][...] = _dot(x, wcb_ref[:, cs]) * _dot(x, whb_ref[:, cs])
        raw_gate[slot][...] = _dot(x, wbb_ref[:, cs])

    base = _traced_zero()

    def finish(unit, slot):
        rows, cs = unit
        for r0 in range(0, UNIT_ROWS, ROW_CHUNK):
            conv = _conv_rows(raw_p[slot], tail_ref, cs, base, r0, cw_ref)
            gate = _raw_rows(raw_gate[slot], base, r0, ROW_CHUNK)
            y_ref[rows.start + r0:rows.start + r0 + ROW_CHUNK, cs] = (gate * conv).astype(y_ref.dtype)
        tail_ref[:, cs] = _raw_rows(raw_p[slot], base, UNIT_ROWS - SUBLANES, SUBLANES)

    _pipeline_units(_units(*y_ref.shape), issue, finish)


def _sc_in(xn, w_in, conv_w, layer, tm=2048, tn=512):
    s, d = xn.shape
    c = conv_w.shape[-1]
    nb = c // tn
    part = lambda k: (lambda n: n + k * nb)
    return pl.pallas_call(
        _sc_in_body,
        grid=(nb, s // tm),
        in_specs=[pl.BlockSpec((tm, d), lambda n, m: (m, 0)),
                  _weight_spec(layer, d, tn, part(0)),
                  _weight_spec(layer, d, tn, part(1)),
                  _weight_spec(layer, d, tn, part(2)),
                  _vec_spec(layer, CONV_TAPS, tn, part(0))],
        out_specs=pl.BlockSpec((tm, tn), lambda n, m: (m, n)),
        out_shape=jax.ShapeDtypeStruct((s, c), BF16),
        scratch_shapes=[pltpu.VMEM((d, tn), BF16)] * 3 + [pltpu.VMEM((SUBLANES, tn), F32)] + _raw_slots(4),
        compiler_params=_params(2),
        name="sc_in",
    )(xn, w_in, w_in, w_in, conv_w)


def _ffn_up_body(x_ref, wg_ref, wv_ref, cwg_ref, cwv_ref, bg_ref, bv_ref, a_ref,
                 wgb_ref, wvb_ref, tailg_ref, tailv_ref, g0_ref, g1_ref, v0_ref, v1_ref):
    _cast_weights_once([(wg_ref, wgb_ref), (wv_ref, wvb_ref)])
    _reset_tails([tailg_ref, tailv_ref])
    raw_g, raw_v = (g0_ref, g1_ref), (v0_ref, v1_ref)

    def issue(unit, slot):
        rows, cs = unit
        x = x_ref[rows, :]
        raw_g[slot][...] = _dot(x, wgb_ref[:, cs])
        raw_v[slot][...] = _dot(x, wvb_ref[:, cs])

    base = _traced_zero()

    def finish(unit, slot):
        rows, cs = unit
        for r0 in range(0, UNIT_ROWS, ROW_CHUNK):
            cg = _conv_rows(raw_g[slot], tailg_ref, cs, base, r0, cwg_ref) + bg_ref[:, cs]
            cv = _conv_rows(raw_v[slot], tailv_ref, cs, base, r0, cwv_ref) + bv_ref[:, cs]
            a_ref[rows.start + r0:rows.start + r0 + ROW_CHUNK, cs] = (
                cg * jax.nn.sigmoid(cg) * cv).astype(a_ref.dtype)
        tailg_ref[:, cs] = _raw_rows(raw_g[slot], base, UNIT_ROWS - SUBLANES, SUBLANES)
        tailv_ref[:, cs] = _raw_rows(raw_v[slot], base, UNIT_ROWS - SUBLANES, SUBLANES)

    _pipeline_units(_units(*a_ref.shape), issue, finish)


def _ffn_up(xn, w_up, conv_w, conv_b, layer, tm=2048, tn=512):
    s, d = xn.shape
    f = w_up.shape[-1] // 2
    nb = f // tn
    part = lambda k: (lambda n: n + k * nb)
    conv_b = conv_b.reshape(conv_b.shape[0], 1, 2 * f)
    return pl.pallas_call(
        _ffn_up_body,
        grid=(nb, s // tm),
        in_specs=[pl.BlockSpec((tm, d), lambda n, m: (m, 0)),
                  _weight_spec(layer, d, tn, part(0)), _weight_spec(layer, d, tn, part(1)),
                  _vec_spec(layer, CONV_TAPS, tn, part(0)), _vec_spec(layer, CONV_TAPS, tn, part(1)),
                  _vec_spec(layer, 1, tn, part(0)), _vec_spec(layer, 1, tn, part(1))],
        out_specs=pl.BlockSpec((tm, tn), lambda n, m: (m, n)),
        out_shape=jax.ShapeDtypeStruct((s, f), BF16),
        scratch_shapes=([pltpu.VMEM((d, tn), BF16)] * 2 + [pltpu.VMEM((SUBLANES, tn), F32)] * 2
                        + _raw_slots(4)),
        compiler_params=_params(2),
        name="ffn_up",
    )(xn, w_up, w_up, conv_w, conv_w, conv_b, conv_b)


def _proj_body(x_ref, w_ref, o_ref, wb_ref, r0_ref, r1_ref, *, epilogue, scaled_blocks, scale):
    _cast_weights_once([(w_ref, wb_ref)])
    raw = (r0_ref, r1_ref)
    if epilogue == "qscale":
        factor = jnp.where(pl.program_id(0) < scaled_blocks, scale, 1.0).astype(F32)

    def issue(unit, slot):
        rows, cs = unit
        raw[slot][...] = _dot(x_ref[rows, :], wb_ref[:, cs])

    base = _traced_zero()

    def finish(unit, slot):
        rows, cs = unit
        for r0 in range(0, UNIT_ROWS, ROW_CHUNK):
            acc = _raw_rows(raw[slot], base, r0, ROW_CHUNK)
            if epilogue == "gelu":
                acc = 0.5 * acc * (1.0 + lax.erf(acc * (1.0 / math.sqrt(2.0))))
            elif epilogue == "qscale":
                acc = acc * factor
            o_ref[rows.start + r0:rows.start + r0 + ROW_CHUNK, cs] = acc.astype(o_ref.dtype)

    _pipeline_units(_units(*o_ref.shape, cols=PAIR_COLS), issue, finish)


def _proj(xn, w, layer, epilogue, *, scaled_cols=0, scale=1.0, tm=2048, tn=1024, name):
    s, d = xn.shape
    n = w.shape[-1]
    body = functools.partial(_proj_body, epilogue=epilogue, scaled_blocks=scaled_cols // tn, scale=scale)
    return pl.pallas_call(
        body,
        grid=(n // tn, s // tm),
        in_specs=[pl.BlockSpec((tm, d), lambda j, m: (m, 0)),
                  _weight_spec(layer, d, tn, lambda j: j)],
        out_specs=pl.BlockSpec((tm, tn), lambda j, m: (m, j)),
        out_shape=jax.ShapeDtypeStruct((s, n), BF16),
        scratch_shapes=[pltpu.VMEM((d, tn), BF16)] + _raw_slots(2, cols=PAIR_COLS),
        compiler_params=_params(2),
        name=name,
    )(xn, w)


WEIGHT_STAGE_ROWS = 512
OUT_PIECE_ROWS = 128


def _stage_weights(w_hbm, layer, wb_ref, stage_ref, sem):
    n = wb_ref.shape[0] // WEIGHT_STAGE_ROWS

    def copy(c):
        return pltpu.make_async_copy(
            w_hbm.at[layer, pl.ds(c * WEIGHT_STAGE_ROWS, WEIGHT_STAGE_ROWS), :],
            stage_ref.at[c % 2], sem.at[c % 2])

    copy(0).start()
    for c in range(n):
        if c + 1 < n:
            copy(c + 1).start()
        copy(c).wait()
        wb_ref[c * WEIGHT_STAGE_ROWS:(c + 1) * WEIGHT_STAGE_ROWS, :] = stage_ref[c % 2].astype(wb_ref.dtype)


def _out_body(a_ref, w_hbm, h_ref, gp_ref, *rest, layer, emit_next):
    if emit_next:
        gn_ref, ho_ref, xn_ref, wb_ref, stage_ref, sem = rest
    else:
        ho_ref, wb_ref, stage_ref, sem = rest

    @pl.when(pl.program_id(0) == 0)
    def _():
        _stage_weights(w_hbm, layer, wb_ref, stage_ref, sem)

    piece = OUT_PIECE_ROWS if emit_next else ho_ref.shape[0]
    for r in range(0, ho_ref.shape[0], piece):
        rows = slice(r, r + piece)
        h = h_ref[rows, :] + _rms(_dot(a_ref[rows, :], wb_ref[...]), gp_ref[...])
        ho_ref[rows, :] = h
        if emit_next:
            xn_ref[rows, :] = _rms(h, gn_ref[...]).astype(xn_ref.dtype)


def _out_proj(a, w, layer, h, g_post, g_next, tm, name):
    s, k = a.shape
    d = w.shape[-1]
    emit_next = g_next is not None
    row = lambda width: pl.BlockSpec((tm, width), lambda i: (i, 0))
    vec = pl.BlockSpec((1, d), lambda i: (0, 0))
    in_specs = [row(k), pl.BlockSpec(memory_space=pl.ANY), row(d), vec]
    args = [a, w, h, g_post.reshape(1, d)]
    out_specs, out_shape = [row(d)], [jax.ShapeDtypeStruct((s, d), F32)]
    if emit_next:
        in_specs.append(vec)
        args.append(g_next.reshape(1, d))
        out_specs.append(row(d))
        out_shape.append(jax.ShapeDtypeStruct((s, d), BF16))
    outs = pl.pallas_call(
        functools.partial(_out_body, layer=layer, emit_next=emit_next),
        grid=(s // tm,),
        in_specs=in_specs,
        out_specs=out_specs,
        out_shape=out_shape,
        scratch_shapes=[pltpu.VMEM((k, d), BF16),
                        pltpu.VMEM((2, WEIGHT_STAGE_ROWS, d), F32),
                        pltpu.SemaphoreType.DMA((2,))],
        compiler_params=_params(1),
        name=name if emit_next else name + "_last",
    )(*args)
    return (outs[0], outs[1]) if emit_next else (outs[0], None)


def _sgu_body(u_ref, v_ref, g_ref, b_ref, ws_ref, bias_ref, y_ref):
    v = v_ref[...].astype(F32)
    mu = jnp.mean(v, axis=-1, keepdims=True)
    vc = v - mu
    var = jnp.mean(vc * vc, axis=-1, keepdims=True)
    vn = (vc * lax.rsqrt(var + EPS) * g_ref[...] + b_ref[...]).astype(BF16)
    t = SGU_CHUNK
    gw = v.shape[1] // SGU_GROUPS
    tril = (lax.broadcasted_iota(jnp.int32, (t, t), 0) >= lax.broadcasted_iota(jnp.int32, (t, t), 1))
    for g in range(SGU_GROUPS):
        ws = jnp.where(tril, ws_ref[g], 0.0).astype(BF16)
        cols = slice(g * gw, (g + 1) * gw)
        for c in range(v.shape[0] // t):
            rows = slice(c * t, (c + 1) * t)
            mixed = _dot(ws, vn[rows, cols]) + bias_ref[:, cols]
            y_ref[rows, cols] = (u_ref[rows, cols].astype(F32) * mixed).astype(y_ref.dtype)


def _sgu(h, ln_g, ln_b, w_s, bias_full, tm=1024):
    s, two_d = h.shape
    d = two_d // 2
    vec = pl.BlockSpec((1, d), lambda m: (0, 0))
    return pl.pallas_call(
        _sgu_body,
        grid=(s // tm,),
        in_specs=[pl.BlockSpec((tm, d), lambda m: (m, 0)),
                  pl.BlockSpec((tm, d), lambda m: (m, 1)),
                  vec, vec,
                  pl.BlockSpec(w_s.shape, lambda m: (0, 0, 0)),
                  pl.BlockSpec(bias_full.shape, lambda m: (0, 0))],
        out_specs=pl.BlockSpec((tm, d), lambda m: (m, 0)),
        out_shape=jax.ShapeDtypeStruct((s, d), BF16),
        compiler_params=_params(1),
        name="sgu_mix",
    )(h, h, ln_g.reshape(1, d), ln_b.reshape(1, d), w_s, bias_full)


class _SbVisit:
    def __init__(self, q_ref, k_ref, v_ref, uo, hd, mask):
        self.cols = slice(hd * SB_HEAD_DIM, (hd + 1) * SB_HEAD_DIM)
        self.q_ref, self.k_ref, self.v_ref, self.uo, self.hd, self.mask = q_ref, k_ref, v_ref, uo, hd, mask

    def scores(self):
        q = self.q_ref[:, self.cols]
        kb = self.k_ref[:, self.cols]
        self.z = lax.dot_general(q, kb, (((1,), (1,)), ((), ())), preferred_element_type=F32)

    def log_terms(self):
        z = self.z
        self.zl = jnp.minimum(z, 0.0) - jnp.log(1.0 + jnp.exp(-jnp.abs(z)))
        l = self.zl - z
        if self.mask is not None:
            l = jnp.where(self.mask, l, 0.0)
        self.l = l.astype(BF16)

    def tile_sums(self):
        self.sums = _dot(self.l, self.uo)

    def weights(self, carry):
        t = SB_TILE
        a = jnp.exp(self.zl + self.sums[:, :t] + jnp.tile(carry, (1, t // LANES)))
        if self.mask is not None:
            a = jnp.where(self.mask, a, 0.0)
        self.a = a.astype(BF16)
        return carry + self.sums[:, t:]

    def values(self, acc):
        return acc + _dot(self.a, self.v_ref[:, self.cols])

    STAGES = 5

    def run_stage(self, s, acc, carry):
        if s == 0:
            self.scores()
        elif s == 1:
            self.log_terms()
        elif s == 2:
            self.tile_sums()
        elif s == 3:
            carry[self.hd] = self.weights(carry[self.hd])
        else:
            acc[self.hd] = self.values(acc[self.hd])


def _sb_pipeline(visits, acc, carry):
    for step in range(len(visits) + _SbVisit.STAGES - 1):
        for s in reversed(range(_SbVisit.STAGES)):
            v = step - s
            if 0 <= v < len(visits):
                visits[v].run_stage(s, acc, carry)


def _sb_attn_body(q_ref, kd_ref, kp_ref, vd_ref, vp_ref, qkv_hbm, o_ref,
                  acc_ref, carry_ref, kbuf_ref, vbuf_ref, sem, *, n_groups):
    t = SB_TILE
    g = pl.program_id(0)
    i = pl.program_id(1)
    r = lax.broadcasted_iota(jnp.int32, (t, t), 0)
    c = lax.broadcasted_iota(jnp.int32, (t, t), 1)
    causal = c < r
    ur = lax.broadcasted_iota(jnp.int32, (t, t + LANES), 0)
    uc = lax.broadcasted_iota(jnp.int32, (t, t + LANES), 1)
    uo = jnp.where((ur > uc) | (uc >= t), 1.0, 0.0).astype(BF16)
    heads = range(SB_HEAD_GROUP)
    zeros = jnp.zeros((t, SB_HEAD_DIM), F32)

    def visit_tiles(tiles, acc, carry):
        visits = [_SbVisit(q_ref, k, v, uo, hd, mask) for k, v, mask in tiles for hd in heads]
        _sb_pipeline(visits, acc, carry)
        for hd in heads:
            acc_ref[hd], carry_ref[hd] = acc[hd], carry[hd]

    def fresh():
        return {hd: zeros for hd in heads}, {hd: zeros for hd in heads}

    @pl.when(i > 0)
    def _():
        visit_tiles([(kd_ref, vd_ref, causal), (kp_ref, vp_ref, None)], *fresh())

    @pl.when(i == 0)
    def _():
        visit_tiles([(kd_ref, vd_ref, causal)], *fresh())

    def fetch(j, part, buf_ref, slot):
        gw = buf_ref.shape[1]
        return pltpu.make_async_copy(
            qkv_hbm.at[pl.ds(pl.multiple_of(j * t, t), t),
                       pl.ds(pl.multiple_of((part * n_groups + g) * gw, LANES), gw)],
            buf_ref, sem.at[slot])

    def cond(state):
        j, cmax = state
        return jnp.logical_and(j >= 0, cmax > -SB_EXP_ZERO)

    def body(state):
        j, _ = state
        copies = [fetch(j, 1, kbuf_ref, 0), fetch(j, 2, vbuf_ref, 1)]
        for c in copies:
            c.start()
        for c in copies:
            c.wait()
        visit_tiles([(kbuf_ref, vbuf_ref, None)],
                    {hd: acc_ref[hd] for hd in heads}, {hd: carry_ref[hd] for hd in heads})
        return j - 1, jnp.max(carry_ref[...])

    lax.while_loop(cond, body, (i - 2, jnp.max(carry_ref[...])))
    for hd in heads:
        o_ref[:, hd * SB_HEAD_DIM:(hd + 1) * SB_HEAD_DIM] = acc_ref[hd].astype(o_ref.dtype)


def _sb_attn(qkv):
    s = qkv.shape[0]
    t = SB_TILE
    gw = SB_HEAD_GROUP * SB_HEAD_DIM
    ng = SB_HEADS // SB_HEAD_GROUP
    diag = lambda part: pl.BlockSpec((t, gw), lambda g, i: (i, part * ng + g))
    prev = lambda part: pl.BlockSpec((t, gw), lambda g, i: (jnp.maximum(i - 1, 0), part * ng + g))
    return pl.pallas_call(
        functools.partial(_sb_attn_body, n_groups=ng),
        grid=(ng, s // t),
        in_specs=[diag(0), diag(1), prev(1), diag(2), prev(2), pl.BlockSpec(memory_space=pl.ANY)],
        out_specs=pl.BlockSpec((t, gw), lambda g, i: (i, g)),
        out_shape=jax.ShapeDtypeStruct((s, SB_HEADS * SB_HEAD_DIM), BF16),
        scratch_shapes=[pltpu.VMEM((SB_HEAD_GROUP, t, SB_HEAD_DIM), F32),
                        pltpu.VMEM((SB_HEAD_GROUP, t, LANES), F32),
                        pltpu.VMEM((t, gw), BF16), pltpu.VMEM((t, gw), BF16),
                        pltpu.SemaphoreType.DMA((2,))],
        compiler_params=_params(2),
        name="sb_attn",
    )(qkv, qkv, qkv, qkv, qkv, qkv)


def kernel(x, norm_mix_pre, norm_mix_post, norm_ffn_pre, norm_ffn_post, sc_w_in, sc_conv_w, sc_w_out, sg_w_in, sg_ln_g, sg_ln_b, sg_w_s, sg_b_s, sg_w_out, sb_w_qkv, sb_w_out, ffn_w_up, ffn_conv_w, ffn_conv_b, ffn_w_down):
    batch, seq, d = x.shape
    assert batch == 1, "row tiles carry the causal conv state across the flattened sequence"
    depth = norm_mix_pre.shape[0]
    n_mixers = 3

    h = x.reshape(seq, d)
    xn = _rmsnorm(h, norm_mix_pre[0])
    for i in range(depth):
        kind, j = i % n_mixers, i // n_mixers
        if kind == 0:
            a = _sc_in(xn, sc_w_in, sc_conv_w, j)
            w_out = sc_w_out
        elif kind == 1:
            hid = _proj(xn, sg_w_in, j, "gelu", name="sg_in")
            bias_full = jnp.repeat(sg_b_s[j].T, hid.shape[1] // 2 // SGU_GROUPS, axis=1)
            a = _sgu(hid, sg_ln_g[j], sg_ln_b[j], sg_w_s[j], bias_full)
            w_out = sg_w_out
        else:
            qkv = _proj(xn, sb_w_qkv, j, "qscale", scaled_cols=SB_HEADS * SB_HEAD_DIM,
                        scale=SB_HEAD_DIM ** -0.5, name="sb_qkv")
            a = _sb_attn(qkv)
            w_out = sb_w_out
        h, xn = _out_proj(a, w_out, j, h, norm_mix_post[i], norm_ffn_pre[i], 512, "mix_out")
        act = _ffn_up(xn, ffn_w_up, ffn_conv_w, ffn_conv_b, i)
        g_next = norm_mix_pre[i + 1] if i + 1 < depth else None
        h, xn = _out_proj(act, ffn_w_down, i, h, norm_ffn_post[i], g_next, 256, "ffn_down")
    return h.reshape(batch, seq, d)
```

```python
import functools
import math

import jax
import jax.numpy as jnp
from jax import lax
from jax.experimental import pallas as pl
from jax.experimental.pallas import tpu as pltpu

EPS = 1e-6
CONV_TAPS = 3
SUBLANES = 8
LANES = 128
MXU_COLS = 256
PAIR_COLS = 2 * MXU_COLS
UNIT_ROWS = 512
ROW_CHUNK = 64
SGU_CHUNK = 128
SGU_GROUPS = 8
SB_HEADS = 16
SB_HEAD_DIM = 128
SB_TILE = 256
SB_HEAD_GROUP = 16
SB_EXP_ZERO = 106.0
MIB = 1024 * 1024
VMEM_LIMIT = 56 * MIB

F32 = jnp.float32
BF16 = jnp.bfloat16


def _params(n_axes, vmem=VMEM_LIMIT):
    return pltpu.CompilerParams(
        dimension_semantics=("arbitrary",) * n_axes, vmem_limit_bytes=vmem)


def _dot(a, b):
    return jnp.dot(a, b, preferred_element_type=F32)


def _rms(x, g):
    ms = jnp.mean(x * x, axis=-1, keepdims=True)
    return x * lax.rsqrt(ms + EPS) * g


def _rmsnorm_body(x_ref, g_ref, o_ref):
    o_ref[...] = _rms(x_ref[...], g_ref[...]).astype(o_ref.dtype)


def _rmsnorm(x, g, tm=1024):
    s, d = x.shape
    return pl.pallas_call(
        _rmsnorm_body,
        grid=(s // tm,),
        in_specs=[pl.BlockSpec((tm, d), lambda m: (m, 0)),
                  pl.BlockSpec((1, d), lambda m: (0, 0))],
        out_specs=pl.BlockSpec((tm, d), lambda m: (m, 0)),
        out_shape=jax.ShapeDtypeStruct((s, d), BF16),
        compiler_params=_params(1),
        name="rmsnorm_first",
    )(x, g.reshape(1, d))


def _weight_spec(layer, d, tn, col_block):
    return pl.BlockSpec((None, d, tn), lambda n, m: (layer, 0, col_block(n)))


def _vec_spec(layer, rows, tn, col_block):
    return pl.BlockSpec((None, rows, tn), lambda n, m: (layer, 0, col_block(n)))


def _cast_weights_once(pairs):
    @pl.when(pl.program_id(1) == 0)
    def _():
        for src, dst in pairs:
            dst[...] = src[...].astype(dst.dtype)


def _units(tm, tn, cols=MXU_COLS):
    return [(slice(r, r + UNIT_ROWS), slice(c, c + cols))
            for c in range(0, tn, cols) for r in range(0, tm, UNIT_ROWS)]


def _pipeline_units(units, issue, finish):
    issue(units[0], 0)
    for k, unit in enumerate(units):
        if k + 1 < len(units):
            issue(units[k + 1], (k + 1) % 2)
        finish(unit, k % 2)


def _traced_zero():
    return jnp.minimum(pl.program_id(1), 0)


def _raw_rows(raw_ref, base, r0, n):
    return raw_ref[pl.ds(pl.multiple_of(base + r0, SUBLANES), n), :]


def _conv_rows(raw_ref, tail_ref, cs, base, r0, w_ref):
    cur = _raw_rows(raw_ref, base, r0, ROW_CHUNK)
    prev = tail_ref[:, cs] if r0 == 0 else _raw_rows(raw_ref, base, r0 - SUBLANES, SUBLANES)
    ext = jnp.concatenate([prev, cur], axis=0)
    s1 = pltpu.roll(ext, 1, 0)[SUBLANES:]
    s2 = pltpu.roll(ext, 2, 0)[SUBLANES:]
    return w_ref[2:3, cs] * cur + w_ref[1:2, cs] * s1 + w_ref[0:1, cs] * s2


def _reset_tails(tail_refs):
    @pl.when(pl.program_id(1) == 0)
    def _():
        for t in tail_refs:
            t[...] = jnp.zeros(t.shape, t.dtype)


def _raw_slots(n, cols=MXU_COLS):
    return [pltpu.VMEM((UNIT_ROWS, cols), F32)] * n


def _sc_in_body(x_ref, wb_ref, wc_ref, wh_ref, cw_ref, y_ref, wbb_ref, wcb_ref, whb_ref, tail_ref,
                p0_ref, p1_ref, g0_ref, g1_ref):
    _cast_weights_once([(wb_ref, wbb_ref), (wc_ref, wcb_ref), (wh_ref, whb_ref)])
    _reset_tails([tail_ref])
    raw_p, raw_gate = (p0_ref, p1_ref), (g0_ref, g1_ref)

    def issue(unit, slot):
        rows, cs = unit
        x = x_ref[rows, :]
        raw_p[slot][...] = _dot(x, wcb_ref[:, cs]) * _dot(x, whb_ref[:, cs])
        raw_gate[slot][...] = _dot(x, wbb_ref[:, cs])

    base = _traced_zero()

    def finish(unit, slot):
        rows, cs = unit
        for r0 in range(0, UNIT_ROWS, ROW_CHUNK):
            conv = _conv_rows(raw_p[slot], tail_ref, cs, base, r0, cw_ref)
            gate = _raw_rows(raw_gate[slot], base, r0, ROW_CHUNK)
            y_ref[rows.start + r0:rows.start + r0 + ROW_CHUNK, cs] = (gate * conv).astype(y_ref.dtype)
        tail_ref[:, cs] = _raw_rows(raw_p[slot], base, UNIT_ROWS - SUBLANES, SUBLANES)

    _pipeline_units(_units(*y_ref.shape), issue, finish)


def _sc_in(xn, w_in, conv_w, layer, tm=2048, tn=512):
    s, d = xn.shape
    c = conv_w.shape[-1]
    nb = c // tn
    part = lambda k: (lambda n: n + k * nb)
    return pl.pallas_call(
        _sc_in_body,
        grid=(nb, s // tm),
        in_specs=[pl.BlockSpec((tm, d), lambda n, m: (m, 0)),
                  _weight_spec(layer, d, tn, part(0)),
                  _weight_spec(layer, d, tn, part(1)),
                  _weight_spec(layer, d, tn, part(2)),
                  _vec_spec(layer, CONV_TAPS, tn, part(0))],
        out_specs=pl.BlockSpec((tm, tn), lambda n, m: (m, n)),
        out_shape=jax.ShapeDtypeStruct((s, c), BF16),
        scratch_shapes=[pltpu.VMEM((d, tn), BF16)] * 3 + [pltpu.VMEM((SUBLANES, tn), F32)] + _raw_slots(4),
        compiler_params=_params(2),
        name="sc_in",
    )(xn, w_in, w_in, w_in, conv_w)


def _ffn_up_body(x_ref, wg_ref, wv_ref, cwg_ref, cwv_ref, bg_ref, bv_ref, a_ref,
                 wgb_ref, wvb_ref, tailg_ref, tailv_ref, g0_ref, g1_ref, v0_ref, v1_ref):
    _cast_weights_once([(wg_ref, wgb_ref), (wv_ref, wvb_ref)])
    _reset_tails([tailg_ref, tailv_ref])
    raw_g, raw_v = (g0_ref, g1_ref), (v0_ref, v1_ref)

    def issue(unit, slot):
        rows, cs = unit
        x = x_ref[rows, :]
        raw_g[slot][...] = _dot(x, wgb_ref[:, cs])
        raw_v[slot][...] = _dot(x, wvb_ref[:, cs])

    base = _traced_zero()

    def finish(unit, slot):
        rows, cs = unit
        for r0 in range(0, UNIT_ROWS, ROW_CHUNK):
            cg = _conv_rows(raw_g[slot], tailg_ref, cs, base, r0, cwg_ref) + bg_ref[:, cs]
            cv = _conv_rows(raw_v[slot], tailv_ref, cs, base, r0, cwv_ref) + bv_ref[:, cs]
            a_ref[rows.start + r0:rows.start + r0 + ROW_CHUNK, cs] = (
                cg * jax.nn.sigmoid(cg) * cv).astype(a_ref.dtype)
        tailg_ref[:, cs] = _raw_rows(raw_g[slot], base, UNIT_ROWS - SUBLANES, SUBLANES)
        tailv_ref[:, cs] = _raw_rows(raw_v[slot], base, UNIT_ROWS - SUBLANES, SUBLANES)

    _pipeline_units(_units(*a_ref.shape), issue, finish)


def _ffn_up(xn, w_up, conv_w, conv_b, layer, tm=2048, tn=512):
    s, d = xn.shape
    f = w_up.shape[-1] // 2
    nb = f // tn
    part = lambda k: (lambda n: n + k * nb)
    conv_b = conv_b.reshape(conv_b.shape[0], 1, 2 * f)
    return pl.pallas_call(
        _ffn_up_body,
        grid=(nb, s // tm),
        in_specs=[pl.BlockSpec((tm, d), lambda n, m: (m, 0)),
                  _weight_spec(layer, d, tn, part(0)), _weight_spec(layer, d, tn, part(1)),
                  _vec_spec(layer, CONV_TAPS, tn, part(0)), _vec_spec(layer, CONV_TAPS, tn, part(1)),
                  _vec_spec(layer, 1, tn, part(0)), _vec_spec(layer, 1, tn, part(1))],
        out_specs=pl.BlockSpec((tm, tn), lambda n, m: (m, n)),
        out_shape=jax.ShapeDtypeStruct((s, f), BF16),
        scratch_shapes=([pltpu.VMEM((d, tn), BF16)] * 2 + [pltpu.VMEM((SUBLANES, tn), F32)] * 2
                        + _raw_slots(4)),
        compiler_params=_params(2),
        name="ffn_up",
    )(xn, w_up, w_up, conv_w, conv_w, conv_b, conv_b)


def _proj_body(x_ref, w_ref, o_ref, wb_ref, r0_ref, r1_ref, *, epilogue, scaled_blocks, scale):
    _cast_weights_once([(w_ref, wb_ref)])
    raw = (r0_ref, r1_ref)
    if epilogue == "qscale":
        factor = jnp.where(pl.program_id(0) < scaled_blocks, scale, 1.0).astype(F32)

    def issue(unit, slot):
        rows, cs = unit
        raw[slot][...] = _dot(x_ref[rows, :], wb_ref[:, cs])

    base = _traced_zero()

    def finish(unit, slot):
        rows, cs = unit
        for r0 in range(0, UNIT_ROWS, ROW_CHUNK):
            acc = _raw_rows(raw[slot], base, r0, ROW_CHUNK)
            if epilogue == "gelu":
                acc = 0.5 * acc * (1.0 + lax.erf(acc * (1.0 / math.sqrt(2.0))))
            elif epilogue == "qscale":
                acc = acc * factor
            o_ref[rows.start + r0:rows.start + r0 + ROW_CHUNK, cs] = acc.astype(o_ref.dtype)

    _pipeline_units(_units(*o_ref.shape, cols=PAIR_COLS), issue, finish)


def _proj(xn, w, layer, epilogue, *, scaled_cols=0, scale=1.0, tm=2048, tn=1024, name):
    s, d = xn.shape
    n = w.shape[-1]
    body = functools.partial(_proj_body, epilogue=epilogue, scaled_blocks=scaled_cols // tn, scale=scale)
    return pl.pallas_call(
        body,
        grid=(n // tn, s // tm),
        in_specs=[pl.BlockSpec((tm, d), lambda j, m: (m, 0)),
                  _weight_spec(layer, d, tn, lambda j: j)],
        out_specs=pl.BlockSpec((tm, tn), lambda j, m: (m, j)),
        out_shape=jax.ShapeDtypeStruct((s, n), BF16),
        scratch_shapes=[pltpu.VMEM((d, tn), BF16)] + _raw_slots(2, cols=PAIR_COLS),
        compiler_params=_params(2),
        name=name,
    )(xn, w)


WEIGHT_STAGE_ROWS = 512
OUT_PIECE_ROWS = 128


def _stage_weights(w_hbm, layer, wb_ref, stage_ref, sem, a_ref):
    n = wb_ref.shape[0] // WEIGHT_STAGE_ROWS

    def copy(c):
        return pltpu.make_async_copy(
            w_hbm.at[layer, pl.ds(c * WEIGHT_STAGE_ROWS, WEIGHT_STAGE_ROWS), :],
            stage_ref.at[c % 2], sem.at[c % 2])

    copy(0).start()
    acc = None
    for c in range(n):
        if c + 1 < n:
            copy(c + 1).start()
        copy(c).wait()
        ks = slice(c * WEIGHT_STAGE_ROWS, (c + 1) * WEIGHT_STAGE_ROWS)
        wb_ref[ks, :] = stage_ref[c % 2].astype(wb_ref.dtype)
        part = _dot(a_ref[:, ks], wb_ref[ks, :])
        acc = part if acc is None else acc + part
    return acc


def _out_body(a_ref, w_hbm, h_ref, gp_ref, *rest, layer, emit_next):
    if emit_next:
        gn_ref, ho_ref, xn_ref, wb_ref, stage_ref, sem = rest
    else:
        ho_ref, wb_ref, stage_ref, sem = rest

    def finish(rows, m):
        h = h_ref[rows, :] + _rms(m, gp_ref[...])
        ho_ref[rows, :] = h
        if emit_next:
            xn_ref[rows, :] = _rms(h, gn_ref[...]).astype(xn_ref.dtype)

    @pl.when(pl.program_id(0) == 0)
    def _():
        finish(slice(None), _stage_weights(w_hbm, layer, wb_ref, stage_ref, sem, a_ref))

    @pl.when(pl.program_id(0) > 0)
    def _():
        piece = OUT_PIECE_ROWS if emit_next else ho_ref.shape[0]
        for r in range(0, ho_ref.shape[0], piece):
            rows = slice(r, r + piece)
            finish(rows, _dot(a_ref[rows, :], wb_ref[...]))


def _out_proj(a, w, layer, h, g_post, g_next, tm, name):
    s, k = a.shape
    d = w.shape[-1]
    emit_next = g_next is not None
    row = lambda width: pl.BlockSpec((tm, width), lambda i: (i, 0))
    vec = pl.BlockSpec((1, d), lambda i: (0, 0))
    in_specs = [row(k), pl.BlockSpec(memory_space=pl.ANY), row(d), vec]
    args = [a, w, h, g_post.reshape(1, d)]
    out_specs, out_shape = [row(d)], [jax.ShapeDtypeStruct((s, d), F32)]
    if emit_next:
        in_specs.append(vec)
        args.append(g_next.reshape(1, d))
        out_specs.append(row(d))
        out_shape.append(jax.ShapeDtypeStruct((s, d), BF16))
    outs = pl.pallas_call(
        functools.partial(_out_body, layer=layer, emit_next=emit_next),
        grid=(s // tm,),
        in_specs=in_specs,
        out_specs=out_specs,
        out_shape=out_shape,
        scratch_shapes=[pltpu.VMEM((k, d), BF16),
                        pltpu.VMEM((2, WEIGHT_STAGE_ROWS, d), F32),
                        pltpu.SemaphoreType.DMA((2,))],
        compiler_params=_params(1),
        name=name if emit_next else name + "_last",
    )(*args)
    return (outs[0], outs[1]) if emit_next else (outs[0], None)


def _sgu_body(u_ref, v_ref, g_ref, b_ref, ws_ref, bias_ref, y_ref):
    v = v_ref[...].astype(F32)
    mu = jnp.mean(v, axis=-1, keepdims=True)
    vc = v - mu
    var = jnp.mean(vc * vc, axis=-1, keepdims=True)
    vn = (vc * lax.rsqrt(var + EPS) * g_ref[...] + b_ref[...]).astype(BF16)
    t = SGU_CHUNK
    gw = v.shape[1] // SGU_GROUPS
    tril = (lax.broadcasted_iota(jnp.int32, (t, t), 0) >= lax.broadcasted_iota(jnp.int32, (t, t), 1))
    for g in range(SGU_GROUPS):
        ws = jnp.where(tril, ws_ref[g], 0.0).astype(BF16)
        cols = slice(g * gw, (g + 1) * gw)
        for c in range(v.shape[0] // t):
            rows = slice(c * t, (c + 1) * t)
            mixed = _dot(ws, vn[rows, cols]) + bias_ref[:, cols]
            y_ref[rows, cols] = (u_ref[rows, cols].astype(F32) * mixed).astype(y_ref.dtype)


def _sgu(h, ln_g, ln_b, w_s, bias_full, tm=1024):
    s, two_d = h.shape
    d = two_d // 2
    vec = pl.BlockSpec((1, d), lambda m: (0, 0))
    return pl.pallas_call(
        _sgu_body,
        grid=(s // tm,),
        in_specs=[pl.BlockSpec((tm, d), lambda m: (m, 0)),
                  pl.BlockSpec((tm, d), lambda m: (m, 1)),
                  vec, vec,
                  pl.BlockSpec(w_s.shape, lambda m: (0, 0, 0)),
                  pl.BlockSpec(bias_full.shape, lambda m: (0, 0))],
        out_specs=pl.BlockSpec((tm, d), lambda m: (m, 0)),
        out_shape=jax.ShapeDtypeStruct((s, d), BF16),
        compiler_params=_params(1),
        name="sgu_mix",
    )(h, h, ln_g.reshape(1, d), ln_b.reshape(1, d), w_s, bias_full)


class _SbVisit:
    def __init__(self, q_ref, k_ref, v_ref, uo, hd, mask):
        self.cols = slice(hd * SB_HEAD_DIM, (hd + 1) * SB_HEAD_DIM)
        self.q_ref, self.k_ref, self.v_ref, self.uo, self.hd, self.mask = q_ref, k_ref, v_ref, uo, hd, mask

    def scores(self):
        q = self.q_ref[:, self.cols]
        kb = self.k_ref[:, self.cols]
        self.z = lax.dot_general(q, kb, (((1,), (1,)), ((), ())), preferred_element_type=F32)

    def log_terms(self):
        z = self.z
        self.zl = jnp.minimum(z, 0.0) - jnp.log(1.0 + jnp.exp(-jnp.abs(z)))
        l = self.zl - z
        if self.mask is not None:
            l = jnp.where(self.mask, l, 0.0)
        self.l = l.astype(BF16)

    def tile_sums(self):
        self.sums = _dot(self.l, self.uo)

    def weights(self, carry):
        t = SB_TILE
        a = jnp.exp(self.zl + self.sums[:, :t] + jnp.tile(carry, (1, t // LANES)))
        if self.mask is not None:
            a = jnp.where(self.mask, a, 0.0)
        self.a = a.astype(BF16)
        return carry + self.sums[:, t:]

    def values(self, acc):
        return acc + _dot(self.a, self.v_ref[:, self.cols])

    STAGES = 5

    def run_stage(self, s, acc, carry):
        if s == 0:
            self.scores()
        elif s == 1:
            self.log_terms()
        elif s == 2:
            self.tile_sums()
        elif s == 3:
            carry[self.hd] = self.weights(carry[self.hd])
        else:
            acc[self.hd] = self.values(acc[self.hd])


def _sb_pipeline(visits, acc, carry):
    for step in range(len(visits) + _SbVisit.STAGES - 1):
        for s in reversed(range(_SbVisit.STAGES)):
            v = step - s
            if 0 <= v < len(visits):
                visits[v].run_stage(s, acc, carry)


def _sb_attn_body(q_ref, kd_ref, kp_ref, vd_ref, vp_ref, qkv_hbm, o_ref,
                  acc_ref, carry_ref, kbuf_ref, vbuf_ref, sem, *, n_groups):
    t = SB_TILE
    g = pl.program_id(0)
    i = pl.program_id(1)
    r = lax.broadcasted_iota(jnp.int32, (t, t), 0)
    c = lax.broadcasted_iota(jnp.int32, (t, t), 1)
    causal = c < r
    ur = lax.broadcasted_iota(jnp.int32, (t, t + LANES), 0)
    uc = lax.broadcasted_iota(jnp.int32, (t, t + LANES), 1)
    uo = jnp.where((ur > uc) | (uc >= t), 1.0, 0.0).astype(BF16)
    heads = range(SB_HEAD_GROUP)
    zeros = jnp.zeros((t, SB_HEAD_DIM), F32)

    def visit_tiles(tiles, acc, carry):
        visits = [_SbVisit(q_ref, k, v, uo, hd, mask) for k, v, mask in tiles for hd in heads]
        _sb_pipeline(visits, acc, carry)
        for hd in heads:
            acc_ref[hd], carry_ref[hd] = acc[hd], carry[hd]

    def fresh():
        return {hd: zeros for hd in heads}, {hd: zeros for hd in heads}

    @pl.when(i > 0)
    def _():
        visit_tiles([(kd_ref, vd_ref, causal), (kp_ref, vp_ref, None)], *fresh())

    @pl.when(i == 0)
    def _():
        visit_tiles([(kd_ref, vd_ref, causal)], *fresh())

    def fetch(j, part, buf_ref, slot):
        gw = buf_ref.shape[1]
        return pltpu.make_async_copy(
            qkv_hbm.at[pl.ds(pl.multiple_of(j * t, t), t),
                       pl.ds(pl.multiple_of((part * n_groups + g) * gw, LANES), gw)],
            buf_ref, sem.at[slot])

    def cond(state):
        j, cmax = state
        return jnp.logical_and(j >= 0, cmax > -SB_EXP_ZERO)

    def body(state):
        j, _ = state
        copies = [fetch(j, 1, kbuf_ref, 0), fetch(j, 2, vbuf_ref, 1)]
        for c in copies:
            c.start()
        for c in copies:
            c.wait()
        visit_tiles([(kbuf_ref, vbuf_ref, None)],
                    {hd: acc_ref[hd] for hd in heads}, {hd: carry_ref[hd] for hd in heads})
        return j - 1, jnp.max(carry_ref[...])

    lax.while_loop(cond, body, (i - 2, jnp.max(carry_ref[...])))
    for hd in heads:
        o_ref[:, hd * SB_HEAD_DIM:(hd + 1) * SB_HEAD_DIM] = acc_ref[hd].astype(o_ref.dtype)


def _sb_attn(qkv):
    s = qkv.shape[0]
    t = SB_TILE
    gw = SB_HEAD_GROUP * SB_HEAD_DIM
    ng = SB_HEADS // SB_HEAD_GROUP
    diag = lambda part: pl.BlockSpec((t, gw), lambda g, i: (i, part * ng + g))
    prev = lambda part: pl.BlockSpec((t, gw), lambda g, i: (jnp.maximum(i - 1, 0), part * ng + g))
    return pl.pallas_call(
        functools.partial(_sb_attn_body, n_groups=ng),
        grid=(ng, s // t),
        in_specs=[diag(0), diag(1), prev(1), diag(2), prev(2), pl.BlockSpec(memory_space=pl.ANY)],
        out_specs=pl.BlockSpec((t, gw), lambda g, i: (i, g)),
        out_shape=jax.ShapeDtypeStruct((s, SB_HEADS * SB_HEAD_DIM), BF16),
        scratch_shapes=[pltpu.VMEM((SB_HEAD_GROUP, t, SB_HEAD_DIM), F32),
                        pltpu.VMEM((SB_HEAD_GROUP, t, LANES), F32),
                        pltpu.VMEM((t, gw), BF16), pltpu.VMEM((t, gw), BF16),
                        pltpu.SemaphoreType.DMA((2,))],
        compiler_params=_params(2),
        name="sb_attn",
    )(qkv, qkv, qkv, qkv, qkv, qkv)


def kernel(x, norm_mix_pre, norm_mix_post, norm_ffn_pre, norm_ffn_post, sc_w_in, sc_conv_w, sc_w_out, sg_w_in, sg_ln_g, sg_ln_b, sg_w_s, sg_b_s, sg_w_out, sb_w_qkv, sb_w_out, ffn_w_up, ffn_conv_w, ffn_conv_b, ffn_w_down):
    batch, seq, d = x.shape
    assert batch == 1, "row tiles carry the causal conv state across the flattened sequence"
    depth = norm_mix_pre.shape[0]
    n_mixers = 3

    h = x.reshape(seq, d)
    xn = _rmsnorm(h, norm_mix_pre[0])
    for i in range(depth):
        kind, j = i % n_mixers, i // n_mixers
        if kind == 0:
            a = _sc_in(xn, sc_w_in, sc_conv_w, j)
            w_out = sc_w_out
        elif kind == 1:
            hid = _proj(xn, sg_w_in, j, "gelu", name="sg_in")
            bias_full = jnp.repeat(sg_b_s[j].T, hid.shape[1] // 2 // SGU_GROUPS, axis=1)
            a = _sgu(hid, sg_ln_g[j], sg_ln_b[j], sg_w_s[j], bias_full)
            w_out = sg_w_out
        else:
            qkv = _proj(xn, sb_w_qkv, j, "qscale", scaled_cols=SB_HEADS * SB_HEAD_DIM,
                        scale=SB_HEAD_DIM ** -0.5, name="sb_qkv")
            a = _sb_attn(qkv)
            w_out = sb_w_out
        h, xn = _out_proj(a, w_out, j, h, norm_mix_post[i], norm_ffn_pre[i], 512, "mix_out")
        act = _ffn_up(xn, ffn_w_up, ffn_conv_w, ffn_conv_b, i)
        g_next = norm_mix_pre[i + 1] if i + 1 < depth else None
        h, xn = _out_proj(act, ffn_w_down, i, h, norm_ffn_post[i], g_next, 256, "ffn_down")
    return h.reshape(batch, seq, d)
```

```python
import functools
import math

import jax
import jax.numpy as jnp
from jax import lax
from jax.experimental import pallas as pl
from jax.experimental.pallas import tpu as pltpu

EPS = 1e-6
CONV_TAPS = 3
SUBLANES = 8
LANES = 128
MXU_COLS = 256
PAIR_COLS = 2 * MXU_COLS
UNIT_ROWS = 512
ROW_CHUNK = 64
SGU_CHUNK = 128
SGU_GROUPS = 8
SB_HEADS = 16
SB_HEAD_DIM = 128
SB_TILE = 256
SB_HEAD_GROUP = 16
SB_EXP_ZERO = 106.0
MIB = 1024 * 1024
VMEM_LIMIT = 56 * MIB

F32 = jnp.float32
BF16 = jnp.bfloat16


def _params(n_axes, vmem=VMEM_LIMIT):
    return pltpu.CompilerParams(
        dimension_semantics=("arbitrary",) * n_axes, vmem_limit_bytes=vmem)


def _dot(a, b):
    return jnp.dot(a, b, preferred_element_type=F32)


def _rms(x, g):
    ms = jnp.mean(x * x, axis=-1, keepdims=True)
    return x * lax.rsqrt(ms + EPS) * g


def _rmsnorm_body(x_ref, g_ref, o_ref):
    o_ref[...] = _rms(x_ref[...], g_ref[...]).astype(o_ref.dtype)


def _rmsnorm(x, g, tm=1024):
    s, d = x.shape
    return pl.pallas_call(
        _rmsnorm_body,
        grid=(s // tm,),
        in_specs=[pl.BlockSpec((tm, d), lambda m: (m, 0)),
                  pl.BlockSpec((1, d), lambda m: (0, 0))],
        out_specs=pl.BlockSpec((tm, d), lambda m: (m, 0)),
        out_shape=jax.ShapeDtypeStruct((s, d), BF16),
        compiler_params=_params(1),
        name="rmsnorm_first",
    )(x, g.reshape(1, d))


def _weight_spec(layer, d, tn, col_block):
    return pl.BlockSpec((None, d, tn), lambda n, m: (layer, 0, col_block(n)))


def _vec_spec(layer, rows, tn, col_block):
    return pl.BlockSpec((None, rows, tn), lambda n, m: (layer, 0, col_block(n)))


def _cast_weights_once(pairs):
    @pl.when(pl.program_id(1) == 0)
    def _():
        for src, dst in pairs:
            dst[...] = src[...].astype(dst.dtype)


def _units(tm, tn, cols=MXU_COLS):
    return [(slice(r, r + UNIT_ROWS), slice(c, c + cols))
            for c in range(0, tn, cols) for r in range(0, tm, UNIT_ROWS)]


def _pipeline_units(units, issue, finish):
    issue(units[0], 0)
    for k, unit in enumerate(units):
        if k + 1 < len(units):
            issue(units[k + 1], (k + 1) % 2)
        finish(unit, k % 2)


def _traced_zero():
    return jnp.minimum(pl.program_id(1), 0)


def _raw_rows(raw_ref, base, r0, n):
    return raw_ref[pl.ds(pl.multiple_of(base + r0, SUBLANES), n), :]


def _conv_rows(raw_ref, tail_ref, cs, base, r0, w_ref):
    cur = _raw_rows(raw_ref, base, r0, ROW_CHUNK)
    prev = tail_ref[:, cs] if r0 == 0 else _raw_rows(raw_ref, base, r0 - SUBLANES, SUBLANES)
    ext = jnp.concatenate([prev, cur], axis=0)
    s1 = pltpu.roll(ext, 1, 0)[SUBLANES:]
    s2 = pltpu.roll(ext, 2, 0)[SUBLANES:]
    return w_ref[2:3, cs] * cur + w_ref[1:2, cs] * s1 + w_ref[0:1, cs] * s2


def _reset_tails(tail_refs):
    @pl.when(pl.program_id(1) == 0)
    def _():
        for t in tail_refs:
            t[...] = jnp.zeros(t.shape, t.dtype)


def _raw_slots(n, cols=MXU_COLS):
    return [pltpu.VMEM((UNIT_ROWS, cols), F32)] * n


def _sc_in_body(x_ref, wb_ref, wc_ref, wh_ref, cw_ref, y_ref, wbb_ref, wcb_ref, whb_ref, tail_ref,
                p0_ref, p1_ref, g0_ref, g1_ref):
    _cast_weights_once([(wb_ref, wbb_ref), (wc_ref, wcb_ref), (wh_ref, whb_ref)])
    _reset_tails([tail_ref])
    raw_p, raw_gate = (p0_ref, p1_ref), (g0_ref, g1_ref)

    def issue(unit, slot):
        rows, cs = unit
        x = x_ref[rows, :]
        raw_p[slot][...] = _dot(x, wcb_ref[:, cs]) * _dot(x, whb_ref[:, cs])
        raw_gate[slot][...] = _dot(x, wbb_ref[:, cs])

    base = _traced_zero()

    def finish(unit, slot):
        rows, cs = unit
        for r0 in range(0, UNIT_ROWS, ROW_CHUNK):
            conv = _conv_rows(raw_p[slot], tail_ref, cs, base, r0, cw_ref)
            gate = _raw_rows(raw_gate[slot], base, r0, ROW_CHUNK)
            y_ref[rows.start + r0:rows.start + r0 + ROW_CHUNK, cs] = (gate * conv).astype(y_ref.dtype)
        tail_ref[:, cs] = _raw_rows(raw_p[slot], base, UNIT_ROWS - SUBLANES, SUBLANES)

    _pipeline_units(_units(*y_ref.shape), issue, finish)


def _sc_in(xn, w_in, conv_w, layer, tm=2048, tn=512):
    s, d = xn.shape
    c = conv_w.shape[-1]
    nb = c // tn
    part = lambda k: (lambda n: n + k * nb)
    return pl.pallas_call(
        _sc_in_body,
        grid=(nb, s // tm),
        in_specs=[pl.BlockSpec((tm, d), lambda n, m: (m, 0)),
                  _weight_spec(layer, d, tn, part(0)),
                  _weight_spec(layer, d, tn, part(1)),
                  _weight_spec(layer, d, tn, part(2)),
                  _vec_spec(layer, CONV_TAPS, tn, part(0))],
        out_specs=pl.BlockSpec((tm, tn), lambda n, m: (m, n)),
        out_shape=jax.ShapeDtypeStruct((s, c), BF16),
        scratch_shapes=[pltpu.VMEM((d, tn), BF16)] * 3 + [pltpu.VMEM((SUBLANES, tn), F32)] + _raw_slots(4),
        compiler_params=_params(2),
        name="sc_in",
    )(xn, w_in, w_in, w_in, conv_w)


def _ffn_up_body(x_ref, wg_ref, wv_ref, cwg_ref, cwv_ref, bg_ref, bv_ref, a_ref,
                 wgb_ref, wvb_ref, tailg_ref, tailv_ref, g0_ref, g1_ref, v0_ref, v1_ref):
    _cast_weights_once([(wg_ref, wgb_ref), (wv_ref, wvb_ref)])
    _reset_tails([tailg_ref, tailv_ref])
    raw_g, raw_v = (g0_ref, g1_ref), (v0_ref, v1_ref)

    def issue(unit, slot):
        rows, cs = unit
        x = x_ref[rows, :]
        raw_g[slot][...] = _dot(x, wgb_ref[:, cs])
        raw_v[slot][...] = _dot(x, wvb_ref[:, cs])

    base = _traced_zero()

    def finish(unit, slot):
        rows, cs = unit
        for r0 in range(0, UNIT_ROWS, ROW_CHUNK):
            cg = _conv_rows(raw_g[slot], tailg_ref, cs, base, r0, cwg_ref) + bg_ref[:, cs]
            cv = _conv_rows(raw_v[slot], tailv_ref, cs, base, r0, cwv_ref) + bv_ref[:, cs]
            a_ref[rows.start + r0:rows.start + r0 + ROW_CHUNK, cs] = (
                cg * jax.nn.sigmoid(cg) * cv).astype(a_ref.dtype)
        tailg_ref[:, cs] = _raw_rows(raw_g[slot], base, UNIT_ROWS - SUBLANES, SUBLANES)
        tailv_ref[:, cs] = _raw_rows(raw_v[slot], base, UNIT_ROWS - SUBLANES, SUBLANES)

    _pipeline_units(_units(*a_ref.shape), issue, finish)


def _ffn_up(xn, w_up, conv_w, conv_b, layer, tm=2048, tn=512):
    s, d = xn.shape
    f = w_up.shape[-1] // 2
    nb = f // tn
    part = lambda k: (lambda n: n + k * nb)
    conv_b = conv_b.reshape(conv_b.shape[0], 1, 2 * f)
    return pl.pallas_call(
        _ffn_up_body,
        grid=(nb, s // tm),
        in_specs=[pl.BlockSpec((tm, d), lambda n, m: (m, 0)),
                  _weight_spec(layer, d, tn, part(0)), _weight_spec(layer, d, tn, part(1)),
                  _vec_spec(layer, CONV_TAPS, tn, part(0)), _vec_spec(layer, CONV_TAPS, tn, part(1)),
                  _vec_spec(layer, 1, tn, part(0)), _vec_spec(layer, 1, tn, part(1))],
        out_specs=pl.BlockSpec((tm, tn), lambda n, m: (m, n)),
        out_shape=jax.ShapeDtypeStruct((s, f), BF16),
        scratch_shapes=([pltpu.VMEM((d, tn), BF16)] * 2 + [pltpu.VMEM((SUBLANES, tn), F32)] * 2
                        + _raw_slots(4)),
        compiler_params=_params(2),
        name="ffn_up",
    )(xn, w_up, w_up, conv_w, conv_w, conv_b, conv_b)


def _proj_body(x_ref, w_ref, o_ref, wb_ref, r0_ref, r1_ref, *, epilogue, scaled_blocks, scale):
    _cast_weights_once([(w_ref, wb_ref)])
    raw = (r0_ref, r1_ref)
    if epilogue == "qscale":
        factor = jnp.where(pl.program_id(0) < scaled_blocks, scale, 1.0).astype(F32)

    def issue(unit, slot):
        rows, cs = unit
        raw[slot][...] = _dot(x_ref[rows, :], wb_ref[:, cs])

    base = _traced_zero()

    def finish(unit, slot):
        rows, cs = unit
        for r0 in range(0, UNIT_ROWS, ROW_CHUNK):
            acc = _raw_rows(raw[slot], base, r0, ROW_CHUNK)
            if epilogue == "gelu":
                acc = 0.5 * acc * (1.0 + lax.erf(acc * (1.0 / math.sqrt(2.0))))
            elif epilogue == "qscale":
                acc = acc * factor
            o_ref[rows.start + r0:rows.start + r0 + ROW_CHUNK, cs] = acc.astype(o_ref.dtype)

    _pipeline_units(_units(*o_ref.shape, cols=PAIR_COLS), issue, finish)


def _proj(xn, w, layer, epilogue, *, scaled_cols=0, scale=1.0, tm=2048, tn=1024, name):
    s, d = xn.shape
    n = w.shape[-1]
    body = functools.partial(_proj_body, epilogue=epilogue, scaled_blocks=scaled_cols // tn, scale=scale)
    return pl.pallas_call(
        body,
        grid=(n // tn, s // tm),
        in_specs=[pl.BlockSpec((tm, d), lambda j, m: (m, 0)),
                  _weight_spec(layer, d, tn, lambda j: j)],
        out_specs=pl.BlockSpec((tm, tn), lambda j, m: (m, j)),
        out_shape=jax.ShapeDtypeStruct((s, n), BF16),
        scratch_shapes=[pltpu.VMEM((d, tn), BF16)] + _raw_slots(2, cols=PAIR_COLS),
        compiler_params=_params(2),
        name=name,
    )(xn, w)


WEIGHT_STAGE_ROWS = 512
OUT_PIECE_ROWS = 128


def _stage_weights(w_hbm, layer, wb_ref, stage_ref, sem, a_ref):
    n = wb_ref.shape[0] // WEIGHT_STAGE_ROWS

    def copy(c):
        return pltpu.make_async_copy(
            w_hbm.at[layer, pl.ds(c * WEIGHT_STAGE_ROWS, WEIGHT_STAGE_ROWS), :],
            stage_ref.at[c % 2], sem.at[c % 2])

    copy(0).start()
    acc = None
    for c in range(n):
        if c + 1 < n:
            copy(c + 1).start()
        copy(c).wait()
        ks = slice(c * WEIGHT_STAGE_ROWS, (c + 1) * WEIGHT_STAGE_ROWS)
        wb_ref[ks, :] = stage_ref[c % 2].astype(wb_ref.dtype)
        part = _dot(a_ref[:, ks], wb_ref[ks, :])
        acc = part if acc is None else acc + part
    return acc


def _out_body(a_ref, w_hbm, h_ref, gp_ref, *rest, layer, emit_next):
    if emit_next:
        gn_ref, ho_ref, xn_ref, wb_ref, stage_ref, sem = rest
    else:
        ho_ref, wb_ref, stage_ref, sem = rest

    def finish(rows, m):
        h = h_ref[rows, :] + _rms(m, gp_ref[...])
        ho_ref[rows, :] = h
        if emit_next:
            xn_ref[rows, :] = _rms(h, gn_ref[...]).astype(xn_ref.dtype)

    @pl.when(pl.program_id(0) == 0)
    def _():
        finish(slice(None), _stage_weights(w_hbm, layer, wb_ref, stage_ref, sem, a_ref))

    @pl.when(pl.program_id(0) > 0)
    def _():
        piece = OUT_PIECE_ROWS if emit_next else ho_ref.shape[0]
        for r in range(0, ho_ref.shape[0], piece):
            rows = slice(r, r + piece)
            finish(rows, _dot(a_ref[rows, :], wb_ref[...]))


def _out_proj(a, w, layer, h, g_post, g_next, tm, name):
    s, k = a.shape
    d = w.shape[-1]
    emit_next = g_next is not None
    row = lambda width: pl.BlockSpec((tm, width), lambda i: (i, 0))
    vec = pl.BlockSpec((1, d), lambda i: (0, 0))
    in_specs = [row(k), pl.BlockSpec(memory_space=pl.ANY), row(d), vec]
    args = [a, w, h, g_post.reshape(1, d)]
    out_specs, out_shape = [row(d)], [jax.ShapeDtypeStruct((s, d), F32)]
    if emit_next:
        in_specs.append(vec)
        args.append(g_next.reshape(1, d))
        out_specs.append(row(d))
        out_shape.append(jax.ShapeDtypeStruct((s, d), BF16))
    outs = pl.pallas_call(
        functools.partial(_out_body, layer=layer, emit_next=emit_next),
        grid=(s // tm,),
        in_specs=in_specs,
        out_specs=out_specs,
        out_shape=out_shape,
        scratch_shapes=[pltpu.VMEM((k, d), BF16),
                        pltpu.VMEM((2, WEIGHT_STAGE_ROWS, d), F32),
                        pltpu.SemaphoreType.DMA((2,))],
        compiler_params=_params(1),
        name=name if emit_next else name + "_last",
    )(*args)
    return (outs[0], outs[1]) if emit_next else (outs[0], None)


def _sgu_body(u_ref, v_ref, g_ref, b_ref, ws_ref, bias_ref, y_ref):
    v = v_ref[...].astype(F32)
    mu = jnp.mean(v, axis=-1, keepdims=True)
    vc = v - mu
    var = jnp.mean(vc * vc, axis=-1, keepdims=True)
    vn = (vc * lax.rsqrt(var + EPS) * g_ref[...] + b_ref[...]).astype(BF16)
    t = SGU_CHUNK
    gw = v.shape[1] // SGU_GROUPS
    tril = (lax.broadcasted_iota(jnp.int32, (t, t), 0) >= lax.broadcasted_iota(jnp.int32, (t, t), 1))
    for g in range(SGU_GROUPS):
        ws = jnp.where(tril, ws_ref[g], 0.0).astype(BF16)
        cols = slice(g * gw, (g + 1) * gw)
        for c in range(v.shape[0] // t):
            rows = slice(c * t, (c + 1) * t)
            mixed = _dot(ws, vn[rows, cols]) + bias_ref[:, cols]
            y_ref[rows, cols] = (u_ref[rows, cols].astype(F32) * mixed).astype(y_ref.dtype)


def _sgu(h, ln_g, ln_b, w_s, bias_full, tm=1024):
    s, two_d = h.shape
    d = two_d // 2
    vec = pl.BlockSpec((1, d), lambda m: (0, 0))
    return pl.pallas_call(
        _sgu_body,
        grid=(s // tm,),
        in_specs=[pl.BlockSpec((tm, d), lambda m: (m, 0)),
                  pl.BlockSpec((tm, d), lambda m: (m, 1)),
                  vec, vec,
                  pl.BlockSpec(w_s.shape, lambda m: (0, 0, 0)),
                  pl.BlockSpec(bias_full.shape, lambda m: (0, 0))],
        out_specs=pl.BlockSpec((tm, d), lambda m: (m, 0)),
        out_shape=jax.ShapeDtypeStruct((s, d), BF16),
        compiler_params=_params(1),
        name="sgu_mix",
    )(h, h, ln_g.reshape(1, d), ln_b.reshape(1, d), w_s, bias_full)


class _SbVisit:
    def __init__(self, q_ref, k_ref, v_ref, uo, hd, mask):
        self.cols = slice(hd * SB_HEAD_DIM, (hd + 1) * SB_HEAD_DIM)
        self.q_ref, self.k_ref, self.v_ref, self.uo, self.hd, self.mask = q_ref, k_ref, v_ref, uo, hd, mask

    def scores(self):
        q = self.q_ref[:, self.cols]
        kb = self.k_ref[:, self.cols]
        self.z = lax.dot_general(q, kb, (((1,), (1,)), ((), ())), preferred_element_type=F32)

    def log_terms(self):
        z = self.z
        self.zl = jnp.minimum(z, 0.0) - jnp.log(1.0 + jnp.exp(-jnp.abs(z)))
        l = self.zl - z
        if self.mask is not None:
            l = jnp.where(self.mask, l, 0.0)
        self.l = l.astype(BF16)

    def tile_sums(self):
        self.sums = _dot(self.l, self.uo)

    def weights(self, carry):
        t = SB_TILE
        a = jnp.exp(self.zl + self.sums[:, :t] + jnp.tile(carry, (1, t // LANES)))
        if self.mask is not None:
            a = jnp.where(self.mask, a, 0.0)
        self.a = a.astype(BF16)
        return carry + self.sums[:, t:]

    def values(self, acc):
        return acc + _dot(self.a, self.v_ref[:, self.cols])

    STAGES = 5

    def run_stage(self, s, acc, carry):
        if s == 0:
            self.scores()
        elif s == 1:
            self.log_terms()
        elif s == 2:
            self.tile_sums()
        elif s == 3:
            carry[self.hd] = self.weights(carry[self.hd])
        else:
            acc[self.hd] = self.values(acc[self.hd])


def _sb_pipeline(visits, acc, carry):
    for step in range(len(visits) + _SbVisit.STAGES - 1):
        for s in reversed(range(_SbVisit.STAGES)):
            v = step - s
            if 0 <= v < len(visits):
                visits[v].run_stage(s, acc, carry)


def _sb_attn_body(q_ref, kd_ref, kp_ref, vd_ref, vp_ref, qkv_hbm, o_ref,
                  acc_ref, carry_ref, kbuf_ref, vbuf_ref, sem, cmax_ref, *, n_groups):
    t = SB_TILE
    g = pl.program_id(0)
    i = pl.program_id(1)
    r = lax.broadcasted_iota(jnp.int32, (t, t), 0)
    c = lax.broadcasted_iota(jnp.int32, (t, t), 1)
    causal = c < r
    ur = lax.broadcasted_iota(jnp.int32, (t, t + LANES), 0)
    uc = lax.broadcasted_iota(jnp.int32, (t, t + LANES), 1)
    uo = jnp.where((ur > uc) | (uc >= t), 1.0, 0.0).astype(BF16)
    heads = range(SB_HEAD_GROUP)
    zeros = jnp.zeros((t, SB_HEAD_DIM), F32)

    def visit_tiles(tiles, acc, carry):
        visits = [_SbVisit(q_ref, k, v, uo, hd, mask) for k, v, mask in tiles for hd in heads]
        _sb_pipeline(visits, acc, carry)
        worst = None
        for hd in heads:
            acc_ref[hd], carry_ref[hd] = acc[hd], carry[hd]
            o_ref[:, hd * SB_HEAD_DIM:(hd + 1) * SB_HEAD_DIM] = acc[hd].astype(o_ref.dtype)
            worst = carry[hd] if worst is None else jnp.maximum(worst, carry[hd])
        cmax_ref[...] = worst

    def fresh():
        return {hd: zeros for hd in heads}, {hd: zeros for hd in heads}

    @pl.when(i > 0)
    def _():
        visit_tiles([(kd_ref, vd_ref, causal), (kp_ref, vp_ref, None)], *fresh())

    @pl.when(i == 0)
    def _():
        visit_tiles([(kd_ref, vd_ref, causal)], *fresh())

    def fetch(j, part, buf_ref, slot):
        gw = buf_ref.shape[1]
        return pltpu.make_async_copy(
            qkv_hbm.at[pl.ds(pl.multiple_of(j * t, t), t),
                       pl.ds(pl.multiple_of((part * n_groups + g) * gw, LANES), gw)],
            buf_ref, sem.at[slot])

    def cond(state):
        j, cmax = state
        return jnp.logical_and(j >= 0, cmax > -SB_EXP_ZERO)

    def body(state):
        j, _ = state
        copies = [fetch(j, 1, kbuf_ref, 0), fetch(j, 2, vbuf_ref, 1)]
        for c in copies:
            c.start()
        for c in copies:
            c.wait()
        visit_tiles([(kbuf_ref, vbuf_ref, None)],
                    {hd: acc_ref[hd] for hd in heads}, {hd: carry_ref[hd] for hd in heads})
        return j - 1, jnp.max(cmax_ref[...])

    lax.while_loop(cond, body, (i - 2, jnp.max(cmax_ref[...])))


def _sb_attn(qkv):
    s = qkv.shape[0]
    t = SB_TILE
    gw = SB_HEAD_GROUP * SB_HEAD_DIM
    ng = SB_HEADS // SB_HEAD_GROUP
    diag = lambda part: pl.BlockSpec((t, gw), lambda g, i: (i, part * ng + g))
    prev = lambda part: pl.BlockSpec((t, gw), lambda g, i: (jnp.maximum(i - 1, 0), part * ng + g))
    return pl.pallas_call(
        functools.partial(_sb_attn_body, n_groups=ng),
        grid=(ng, s // t),
        in_specs=[diag(0), diag(1), prev(1), diag(2), prev(2), pl.BlockSpec(memory_space=pl.ANY)],
        out_specs=pl.BlockSpec((t, gw), lambda g, i: (i, g)),
        out_shape=jax.ShapeDtypeStruct((s, SB_HEADS * SB_HEAD_DIM), BF16),
        scratch_shapes=[pltpu.VMEM((SB_HEAD_GROUP, t, SB_HEAD_DIM), F32),
                        pltpu.VMEM((SB_HEAD_GROUP, t, LANES), F32),
                        pltpu.VMEM((t, gw), BF16), pltpu.VMEM((t, gw), BF16),
                        pltpu.SemaphoreType.DMA((2,)),
                        pltpu.VMEM((t, LANES), F32)],
        compiler_params=_params(2),
        name="sb_attn",
    )(qkv, qkv, qkv, qkv, qkv, qkv)


def kernel(x, norm_mix_pre, norm_mix_post, norm_ffn_pre, norm_ffn_post, sc_w_in, sc_conv_w, sc_w_out, sg_w_in, sg_ln_g, sg_ln_b, sg_w_s, sg_b_s, sg_w_out, sb_w_qkv, sb_w_out, ffn_w_up, ffn_conv_w, ffn_conv_b, ffn_w_down):
    batch, seq, d = x.shape
    assert batch == 1, "row tiles carry the causal conv state across the flattened sequence"
    depth = norm_mix_pre.shape[0]
    n_mixers = 3

    h = x.reshape(seq, d)
    xn = _rmsnorm(h, norm_mix_pre[0])
    for i in range(depth):
        kind, j = i % n_mixers, i // n_mixers
        if kind == 0:
            a = _sc_in(xn, sc_w_in, sc_conv_w, j)
            w_out = sc_w_out
        elif kind == 1:
            hid = _proj(xn, sg_w_in, j, "gelu", name="sg_in")
            bias_full = jnp.repeat(sg_b_s[j].T, hid.shape[1] // 2 // SGU_GROUPS, axis=1)
            a = _sgu(hid, sg_ln_g[j], sg_ln_b[j], sg_w_s[j], bias_full)
            w_out = sg_w_out
        else:
            qkv = _proj(xn, sb_w_qkv, j, "qscale", scaled_cols=SB_HEADS * SB_HEAD_DIM,
                        scale=SB_HEAD_DIM ** -0.5, name="sb_qkv")
            a = _sb_attn(qkv)
            w_out = sb_w_out
        h, xn = _out_proj(a, w_out, j, h, norm_mix_post[i], norm_ffn_pre[i], 512, "mix_out")
        act = _ffn_up(xn, ffn_w_up, ffn_conv_w, ffn_conv_b, i)
        g_next = norm_mix_pre[i + 1] if i + 1 < depth else None
        h, xn = _out_proj(act, ffn_w_down, i, h, norm_ffn_post[i], g_next, 256, "ffn_down")
    return h.reshape(batch, seq, d)
```
